```python
import jax
import jax.numpy as jnp
from jax import lax
import numpy as np

D_MODEL = 1024
BATCH = 16
SEQ = 4096
DEPTH = 1
DEC_BATCH = 128
DEC_SEQ = 4
PAST_LEN = 8192
PAGE_SIZE = 128

D_MIX = D_MODEL
D_GLA = D_MIX // 2
D_FOX = D_MIX - D_GLA
GLA_H = 4
GLA_DV = D_GLA // GLA_H
GLA_DK = GLA_DV // 2
GLA_RANK = 16
GLA_GATE_NORM = 16.0
GLA_CHUNK = 32
FOX_H = 8
FOX_HD = D_FOX // FOX_H
Q_BLOCK = 128
RMS_EPS = 1e-6
POOL_NUM = 5
POOL_DEN = 4

kernel_name = 'hymba_gla_fox_decode_step'


def _split_sizes():
    return (GLA_H * GLA_DK, GLA_H * GLA_DK, D_GLA, D_GLA, GLA_RANK,
            D_FOX, D_FOX, D_FOX, D_FOX, FOX_H)


def _split_offsets():
    offs, acc = [], 0
    for s in _split_sizes()[:-1]:
        acc += s
        offs.append(acc)
    return offs


def _rmsnorm(x, g):
    xf = x.astype(jnp.float32)
    xf = xf * lax.rsqrt(jnp.mean(xf * xf, axis=-1, keepdims=True) + RMS_EPS)
    return (xf * g.astype(jnp.float32)).astype(x.dtype)


def _project(x, g, w_in, w_a2, b_a, b_f):
    B, T = x.shape[0], x.shape[1]
    h = _rmsnorm(x, g)
    p = h @ w_in
    qg, kg, vg, zg, alr, qf, kf, vf, zf, fl = jnp.split(p, _split_offsets(), axis=-1)
    qg = qg.reshape(B, T, GLA_H, GLA_DK) * (GLA_DK ** -0.5)
    kg = kg.reshape(B, T, GLA_H, GLA_DK)
    vg = vg.reshape(B, T, GLA_H, GLA_DV)
    log_a = jax.nn.log_sigmoid((alr @ w_a2 + b_a).astype(jnp.float32)) / GLA_GATE_NORM
    log_a = log_a.reshape(B, T, GLA_H, GLA_DK)
    qf = qf.reshape(B, T, FOX_H, FOX_HD)
    kf = kf.reshape(B, T, FOX_H, FOX_HD)
    vf = vf.reshape(B, T, FOX_H, FOX_HD)
    logf = jax.nn.log_sigmoid((fl + b_f).astype(jnp.float32))
    return qg, kg, vg, zg, log_a, qf, kf, vf, zf, logf


def _gla_chunk(S0, q, k, v, la):
    C = q.shape[2]
    b = jnp.cumsum(la, axis=2)
    b_last = b[:, :, -1:, :]
    qt = q * jnp.exp(b)
    kt = k * jnp.exp(-b)
    causal = jnp.tril(jnp.ones((C, C), dtype=bool))
    att = jnp.where(causal, jnp.einsum('bhtd,bhsd->bhts', qt, kt), 0.0)
    o = jnp.einsum('bhts,bhsv->bhtv', att, v) + jnp.einsum('bhtd,bhdv->bhtv', qt, S0)
    S = (jnp.exp(b_last[:, :, 0, :])[..., None] * S0
         + jnp.einsum('bhsd,bhsv->bhdv', k * jnp.exp(b_last - b), v))
    return o, S


def _gla_prompt(q, k, v, la):
    B, T = q.shape[0], q.shape[1]
    nc = T // GLA_CHUNK

    def chunks(a):
        return a.astype(jnp.float32).reshape(B, nc, GLA_CHUNK, a.shape[2], a.shape[3]).transpose(1, 0, 3, 2, 4)

    S0 = jnp.zeros((B, GLA_H, GLA_DK, GLA_DV), jnp.float32)

    def step(S, xs):
        o, S = _gla_chunk(S, *xs)
        return S, o

    S, o = lax.scan(step, S0, (chunks(q), chunks(k), chunks(v), chunks(la)))
    o = o.transpose(1, 0, 3, 2, 4).reshape(B, T, GLA_H, GLA_DV)
    return o, S


def _gla_sample(S0, q, k, v, la):
    tr = lambda a: a.astype(jnp.float32).transpose(0, 2, 1, 3)
    o, S = _gla_chunk(S0.astype(jnp.float32), tr(q), tr(k), tr(v), tr(la))
    return o.transpose(0, 2, 1, 3), S


def _gla_out(o, z, gla_norm):
    B, T = o.shape[0], o.shape[1]
    o = o * lax.rsqrt(jnp.mean(o * o, axis=-1, keepdims=True) + RMS_EPS) * gla_norm.astype(jnp.float32)
    return o.reshape(B, T, D_GLA).astype(z.dtype) * jax.nn.silu(z)


def _fox_prompt(q, k, v, logf):
    B, T = q.shape[0], q.shape[1]
    nb = T // Q_BLOCK
    c = jnp.cumsum(logf, axis=1).transpose(0, 2, 1)
    qb = q.reshape(B, nb, Q_BLOCK, FOX_H, FOX_HD).transpose(1, 0, 2, 3, 4)
    cb = c.reshape(B, FOX_H, nb, Q_BLOCK).transpose(2, 0, 1, 3)
    starts = jnp.arange(nb, dtype=jnp.int32) * Q_BLOCK
    kpos = jnp.arange(T, dtype=jnp.int32)
    scale = FOX_HD ** -0.5

    def block(args):
        qi, ci, s0 = args
        s = jnp.einsum('bqhd,bkhd->bhqk', qi, k).astype(jnp.float32) * scale
        s = s + ci[..., :, None] - c[..., None, :]
        qpos = s0 + jnp.arange(Q_BLOCK, dtype=jnp.int32)
        s = jnp.where(kpos[None, :] <= qpos[:, None], s, -jnp.inf)
        p = jax.nn.softmax(s, axis=-1)
        return jnp.einsum('bhqk,bkhd->bqhd', p.astype(v.dtype), v)

    o = lax.map(block, (qb, cb, starts))
    return o.transpose(1, 0, 2, 3, 4).reshape(B, T, D_FOX)


def _fox_sample(q, k, v, logf, ck, cv, clf, page_table):
    B, S = q.shape[0], q.shape[1]
    kp = ck[page_table].reshape(B, -1, FOX_H, FOX_HD)
    vp = cv[page_table].reshape(B, -1, FOX_H, FOX_HD)
    lfp = clf[page_table].reshape(B, -1, FOX_H).astype(jnp.float32)
    P = kp.shape[1]
    suffix = (lax.cumsum(lfp, axis=1, reverse=True) - lfp).transpose(0, 2, 1)
    cn = jnp.cumsum(logf, axis=1).transpose(0, 2, 1)
    scale = FOX_HD ** -0.5
    s_past = (jnp.einsum('bqhd,bkhd->bhqk', q, kp).astype(jnp.float32) * scale
              + cn[..., :, None] + suffix[..., None, :])
    s_new = (jnp.einsum('bqhd,bkhd->bhqk', q, k).astype(jnp.float32) * scale
             + cn[..., :, None] - cn[..., None, :])
    causal = jnp.tril(jnp.ones((S, S), dtype=bool))
    s_new = jnp.where(causal, s_new, -jnp.inf)
    p = jax.nn.softmax(jnp.concatenate([s_past, s_new], axis=-1), axis=-1)
    o = (jnp.einsum('bhqk,bkhd->bqhd', p[..., :P].astype(v.dtype), vp)
         + jnp.einsum('bhqk,bkhd->bqhd', p[..., P:].astype(v.dtype), v))
    return o.reshape(B, S, D_FOX)


def _layer_prompt(x, g, w_in, w_a2, b_a, b_f, gla_norm, w_out):
    qg, kg, vg, zg, log_a, qf, kf, vf, zf, logf = _project(x, g, w_in, w_a2, b_a, b_f)
    o_g, S = _gla_prompt(qg, kg, vg, log_a)
    y_g = _gla_out(o_g, zg, gla_norm)
    y_f = _fox_prompt(qf, kf, vf, logf) * jax.nn.silu(zf)
    x = x + jnp.concatenate([y_g, y_f], axis=-1) @ w_out
    return x, kf, vf, logf, S


def _layer_sample(x, ck, cv, clf, S0, page_table, g, w_in, w_a2, b_a, b_f, gla_norm, w_out):
    qg, kg, vg, zg, log_a, qf, kf, vf, zf, logf = _project(x, g, w_in, w_a2, b_a, b_f)
    o_g, S = _gla_sample(S0, qg, kg, vg, log_a)
    y_g = _gla_out(o_g, zg, gla_norm)
    y_f = _fox_sample(qf, kf, vf, logf, ck, cv, clf, page_table) * jax.nn.silu(zf)
    x = x + jnp.concatenate([y_g, y_f], axis=-1) @ w_out
    return x, kf, vf, logf, S


def setup_inputs(seed: int = 0) -> dict:
    key = jax.random.key(seed)
    ks = jax.random.split(key, 16)
    n_pages = PAST_LEN // PAGE_SIZE
    n_used = DEC_BATCH * n_pages
    n_pool = (POOL_NUM * n_used) // POOL_DEN
    d_in = sum(_split_sizes())
    nrm = jax.random.normal
    x_prompt = nrm(ks[0], (BATCH, SEQ, D_MODEL), jnp.float32)
    x_sample = nrm(ks[1], (DEC_BATCH, DEC_SEQ, D_MODEL), jnp.float32)
    cache_k = nrm(ks[2], (DEPTH, n_pool, PAGE_SIZE, FOX_H, FOX_HD), jnp.float32)
    cache_v = nrm(ks[3], (DEPTH, n_pool, PAGE_SIZE, FOX_H, FOX_HD), jnp.float32)
    cache_logf = jax.nn.log_sigmoid(3.0 + 0.5 * nrm(ks[4], (DEPTH, n_pool, PAGE_SIZE, FOX_H), jnp.float32))
    state_gla = 0.1 * nrm(ks[5], (DEPTH, DEC_BATCH, GLA_H, GLA_DK, GLA_DV), jnp.float32)
    page_table = jax.random.permutation(ks[6], n_pool)[:n_used].reshape(DEC_BATCH, n_pages).astype(jnp.int32)
    g_norm = 1.0 + 0.01 * nrm(ks[7], (DEPTH, D_MODEL), jnp.float32)
    w_in = nrm(ks[8], (DEPTH, D_MODEL, d_in), jnp.float32) * (D_MODEL ** -0.5)
    w_a2 = nrm(ks[9], (DEPTH, GLA_RANK, GLA_H * GLA_DK), jnp.float32) * (GLA_RANK ** -0.5)
    b_a = 0.01 * nrm(ks[10], (DEPTH, GLA_H * GLA_DK), jnp.float32)
    b_f = 3.0 + 0.1 * nrm(ks[11], (DEPTH, FOX_H), jnp.float32)
    gla_norm = 1.0 + 0.01 * nrm(ks[12], (DEPTH, GLA_DV), jnp.float32)
    w_out = nrm(ks[13], (DEPTH, D_MIX, D_MODEL), jnp.float32) * (D_MIX ** -0.5)
    g_final = 1.0 + 0.01 * nrm(ks[14], (D_MODEL,), jnp.float32)
    return {'x_prompt': x_prompt, 'x_sample': x_sample, 'cache_k': cache_k, 'cache_v': cache_v,
            'cache_logf': cache_logf, 'state_gla': state_gla, 'page_table': page_table,
            'g_norm': g_norm, 'w_in': w_in, 'w_a2': w_a2, 'b_a': b_a, 'b_f': b_f,
            'gla_norm': gla_norm, 'w_out': w_out, 'g_final': g_final}


def reference(x_prompt, x_sample, cache_k, cache_v, cache_logf, state_gla, page_table,
              g_norm, w_in, w_a2, b_a, b_f, gla_norm, w_out, g_final):
    xp, xs = x_prompt, x_sample
    kp_l, vp_l, lp_l, sp_l = [], [], [], []
    ks_l, vs_l, ls_l, ss_l = [], [], [], []
    for l in range(DEPTH):
        xp, kp, vp, lfp, Sp = _layer_prompt(xp, g_norm[l], w_in[l], w_a2[l], b_a[l], b_f[l],
                                            gla_norm[l], w_out[l])
        xs, kn, vn, lfn, Sn = _layer_sample(xs, cache_k[l], cache_v[l], cache_logf[l], state_gla[l],
                                            page_table, g_norm[l], w_in[l], w_a2[l], b_a[l], b_f[l],
                                            gla_norm[l], w_out[l])
        kp_l.append(kp); vp_l.append(vp); lp_l.append(lfp); sp_l.append(Sp)
        ks_l.append(kn); vs_l.append(vn); ls_l.append(lfn); ss_l.append(Sn)
    y_prompt = _rmsnorm(xp, g_final)
    y_sample = _rmsnorm(xs, g_final)
    return (y_prompt, y_sample,
            jnp.stack(kp_l), jnp.stack(vp_l), jnp.stack(lp_l), jnp.stack(sp_l),
            jnp.stack(ks_l), jnp.stack(vs_l), jnp.stack(ls_l), jnp.stack(ss_l))
```

```python
import functools
import math

import jax
import jax.numpy as jnp
from jax import lax
from jax.experimental import pallas as pl
from jax.experimental.pallas import tpu as pltpu

F32 = jnp.float32
BF16 = jnp.bfloat16

GLA_H = 4
GLA_DK = 64
GLA_DV = 128
GLA_RANK = 16
GLA_GATE_NORM = 16.0
GLA_CHUNK = 32
FOX_H = 8
FOX_HD = 64
RMS_EPS = 1e-6

D_QK = GLA_H * GLA_DK
D_GLA = GLA_H * GLA_DV
D_FOX = FOX_H * FOX_HD
LANE = 128
SUBLANE = 8
VMEM_LIMIT = 56 * 1024 * 1024

_OFF_QG, _OFF_KG, _OFF_VG, _OFF_ZG = 0, 256, 512, 1024
_OFF_QF, _OFF_KF, _OFF_VF, _OFF_ZF, _OFF_END = 1536, 2048, 2560, 3072, 3584


def _dot(a, b):
    return jnp.dot(a, b, preferred_element_type=F32)


def _dot_nt(a, b):
    return lax.dot_general(a, b, (((1,), (1,)), ((), ())), preferred_element_type=F32)


def _dot_tn(a, b):
    return lax.dot_general(a, b, (((0,), (0,)), ((), ())), preferred_element_type=F32)


def _split3(x):
    hi = x.astype(BF16)
    r = x - hi.astype(F32)
    mid = r.astype(BF16)
    lo = (r - mid.astype(F32)).astype(BF16)
    return hi, mid, lo


def _dot_exact_mask(mask01, x):
    hi, mid, lo = _split3(x)
    return _dot(mask01, hi) + _dot(mask01, mid) + _dot(mask01, lo)


def _log_sigmoid(x):
    return jnp.minimum(x, 0.0) - jnp.log1p(jnp.exp(-jnp.abs(x)))


def _silu(x):
    return x / (1.0 + jnp.exp(-x))


def _log2(n):
    k = n.bit_length() - 1
    assert (1 << k) == n, f"{n} must be a power of two"
    return k


def _params(sem):
    return pltpu.CompilerParams(dimension_semantics=sem, vmem_limit_bytes=VMEM_LIMIT)


def _proj_kernel(x_ref, g_ref, wm_ref, wkvt_ref, ws_ref, wa_ref, ba_ref, bf_ref, *refs, tm, seq_len, names):
    out = dict(zip(names, refs))
    carry_ref = refs[len(names)]
    x = x_ref[...]
    ms = jnp.mean(x * x, axis=-1, keepdims=True)
    h = (x * lax.rsqrt(ms + RMS_EPS) * g_ref[...]).astype(BF16)

    def grp(a, b):
        return _dot(h, wm_ref[:, a:b])

    out["qg"][...] = (grp(_OFF_QG, _OFF_KG) * (GLA_DK ** -0.5)).astype(BF16)
    out["kg"][...] = grp(_OFF_KG, _OFF_VG).astype(BF16)
    out["vg"][...] = grp(_OFF_VG, _OFF_ZG).astype(BF16)
    out["zg"][...] = grp(_OFF_ZG, _OFF_QF).astype(BF16)
    out["qf"][...] = (grp(_OFF_QF, _OFF_KF) * (FOX_HD ** -0.5)).astype(BF16)
    out["zf"][...] = grp(_OFF_ZF, _OFF_END).astype(BF16)

    small = _dot(h, ws_ref[...])
    la = _log_sigmoid(_dot(small.astype(BF16), wa_ref[...]) + ba_ref[...])
    out["la"][...] = la * (1.0 / GLA_GATE_NORM)
    lf = _log_sigmoid(small + bf_ref[...])

    row = lax.broadcasted_iota(jnp.int32, (tm, tm), 0)
    col = lax.broadcasted_iota(jnp.int32, (tm, tm), 1)
    keep = col <= row
    if seq_len < tm:
        sh = _log2(seq_len)
        keep = jnp.logical_and(keep, lax.shift_right_logical(row, sh) == lax.shift_right_logical(col, sh))
    tri = jnp.where(keep, 1.0, 0.0).astype(BF16)
    c = _dot_exact_mask(tri, lf)

    if seq_len >= tm:
        steps = seq_len // tm
        if steps > 1:
            @pl.when(pl.program_id(0) % steps == 0)
            def _():
                carry_ref[...] = jnp.zeros_like(carry_ref)

            c = c + carry_ref[0:1, :]
            carry_ref[0:1, :] = c[tm - 1:tm, :]
        kvt = _dot_nt(wkvt_ref[...], h)
        out["kft"][0] = kvt[0:D_FOX]
        out["kfbt"][0] = kvt[0:D_FOX].astype(BF16)
        out["vft"][0] = kvt[D_FOX:2 * D_FOX]
        out["vfbt"][0] = kvt[D_FOX:2 * D_FOX].astype(BF16)
        out["lft"][0] = lf.T[0:FOX_H, :]
        out["ct"][0] = c.T[0:FOX_H, :]
    else:
        out["kf"][...] = grp(_OFF_KF, _OFF_VF)
        out["vf"][...] = grp(_OFF_VF, _OFF_ZF)
        out["lf"][...] = lf[:, 0:FOX_H]
        out["cc"][...] = c[:, 0:FOX_H]


def _proj(x2d, g, w_main, w_kvt, w_small, w_a2p, b_a, b_fp, *, seq_len, tm):
    n, d = x2d.shape
    assert n % tm == 0 and (seq_len % tm == 0 or tm % seq_len == 0)
    rows = lambda w: pl.BlockSpec((tm, w), lambda i: (i, 0))
    full = lambda a: pl.BlockSpec(a.shape, lambda i: (0, 0))
    tok = lambda w, dt: (jax.ShapeDtypeStruct((n, w), dt), rows(w))
    outs = {"qg": tok(D_QK, BF16), "kg": tok(D_QK, BF16), "vg": tok(D_GLA, BF16), "zg": tok(D_GLA, BF16),
            "la": tok(D_QK, F32), "qf": tok(D_FOX, BF16), "zf": tok(D_FOX, BF16)}
    if seq_len >= tm:
        steps = seq_len // tm
        nseq = n // seq_len
        feat = lambda w, dt: (jax.ShapeDtypeStruct((nseq, w, seq_len), dt),
                              pl.BlockSpec((1, w, tm), lambda i: (i // steps, 0, i % steps)))
        outs.update({"kft": feat(D_FOX, F32), "kfbt": feat(D_FOX, BF16), "vft": feat(D_FOX, F32),
                     "vfbt": feat(D_FOX, BF16), "lft": feat(FOX_H, F32), "ct": feat(FOX_H, F32)})
    else:
        outs.update({"kf": tok(D_FOX, F32), "vf": tok(D_FOX, F32), "lf": tok(FOX_H, F32), "cc": tok(FOX_H, F32)})
    names = tuple(outs)
    res = pl.pallas_call(
        functools.partial(_proj_kernel, tm=tm, seq_len=seq_len, names=names),
        grid=(n // tm,),
        in_specs=[rows(d), full(g), full(w_main), full(w_kvt), full(w_small), full(w_a2p), full(b_a), full(b_fp)],
        out_specs=[outs[k][1] for k in names],
        out_shape=[outs[k][0] for k in names],
        scratch_shapes=[pltpu.VMEM((SUBLANE, LANE), F32)],
        compiler_params=_params(("arbitrary",)),
        name="proj",
    )(x2d, g, w_main, w_kvt, w_small, w_a2p, b_a, b_fp)
    return dict(zip(names, res))


def _gla_prepare(q_ref, k_ref, la_ref, rows, chunk):
    la = la_ref[...].reshape(rows, D_QK)
    sh = _log2(chunk)
    row = lax.broadcasted_iota(jnp.int32, (rows, rows), 0)
    col = lax.broadcasted_iota(jnp.int32, (rows, rows), 1)
    same = lax.shift_right_logical(row, sh) == lax.shift_right_logical(col, sh)
    lower = jnp.logical_and(same, col <= row)
    upper = jnp.logical_and(same, col > row)
    cum = _dot_exact_mask(jnp.where(lower, 1.0, 0.0).astype(BF16), la)
    suf = _dot_exact_mask(jnp.where(upper, 1.0, 0.0).astype(BF16), la)
    q = q_ref[...].reshape(rows, D_QK).astype(F32)
    k = k_ref[...].reshape(rows, D_QK).astype(F32)
    qt = (q * jnp.exp(cum)).astype(BF16)
    kt = (k * jnp.exp(-cum)).astype(BF16)
    kp = (k * jnp.exp(suf)).astype(BF16)
    dec = jnp.exp(cum + suf)
    return qt, kt, kp, dec, lower


def _gla_intra(qtp, ktp, vp, lower):
    lane = lax.broadcasted_iota(jnp.int32, (1, 2 * GLA_DK), 1)
    first = lane < GLA_DK
    zero = jnp.zeros_like(qtp)
    a0 = jnp.where(lower, _dot_nt(jnp.where(first, qtp, zero), ktp), 0.0).astype(BF16)
    a1 = jnp.where(lower, _dot_nt(jnp.where(first, zero, qtp), ktp), 0.0).astype(BF16)
    vlane = lax.broadcasted_iota(jnp.int32, (1, 2 * GLA_DV), 1)
    vzero = jnp.zeros_like(vp)
    v0 = jnp.where(vlane < GLA_DV, vp, vzero)
    v1 = jnp.where(vlane < GLA_DV, vzero, vp)
    return _dot(a0, v0) + _dot(a1, v1)


def _pair_blockdiag_mask():
    r = lax.broadcasted_iota(jnp.int32, (2 * GLA_DV, 2 * GLA_DK), 0)
    c = lax.broadcasted_iota(jnp.int32, (2 * GLA_DV, 2 * GLA_DK), 1)
    return lax.shift_right_logical(r, _log2(GLA_DV)) == lax.shift_right_logical(c, _log2(GLA_DK))


def _gla_finish(o, z_ref, gn_ref, y_ref, rows):
    ys = []
    for h in range(GLA_H):
        oh = o[:, h * GLA_DV:(h + 1) * GLA_DV]
        ms = jnp.mean(oh * oh, axis=-1, keepdims=True)
        ys.append(oh * lax.rsqrt(ms + RMS_EPS) * gn_ref[...])
    y = jnp.concatenate(ys, axis=1) * _silu(z_ref[...].reshape(rows, D_GLA).astype(F32))
    y_ref[...] = y.astype(BF16).reshape(y_ref.shape)


def _state_to_pairs(s_ref, idx, p):
    sa = s_ref[idx + (2 * p,)]
    sb = s_ref[idx + (2 * p + 1,)]
    z = jnp.zeros_like(sa)
    sbd = jnp.concatenate([jnp.concatenate([sa, z], axis=1), jnp.concatenate([z, sb], axis=1)], axis=0)
    return sbd.T


def _pairs_to_state(st, s_ref, idx, p):
    sbd = st.T
    s_ref[idx + (2 * p,)] = sbd[0:GLA_DK, 0:GLA_DV]
    s_ref[idx + (2 * p + 1,)] = sbd[GLA_DK:2 * GLA_DK, GLA_DV:2 * GLA_DV]


def _gla_prompt_kernel(q_ref, k_ref, v_ref, z_ref, la_ref, gn_ref, y_ref, sout_ref, st_ref, *, tb, chunk):
    t = pl.program_id(1)

    @pl.when(t == 0)
    def _():
        st_ref[...] = jnp.zeros_like(st_ref)

    qt, kt, kp, dec, lower = _gla_prepare(q_ref, k_ref, la_ref, tb, chunk)
    v = v_ref[0]
    bd = _pair_blockdiag_mask()
    o_parts = []
    for p in range(GLA_H // 2):
        ql = slice(p * 2 * GLA_DK, (p + 1) * 2 * GLA_DK)
        qtp, ktp, kpp = qt[:, ql], kt[:, ql], kp[:, ql]
        vp = v[:, p * 2 * GLA_DV:(p + 1) * 2 * GLA_DV]
        o_intra = _gla_intra(qtp, ktp, vp, lower)
        s = st_ref[p]
        inter = []
        for c in range(tb // chunk):
            sl = slice(c * chunk, (c + 1) * chunk)
            inter.append(_dot_nt(qtp[sl], s.astype(BF16)))
            upd = _dot_tn(vp[sl], kpp[sl])
            s = s * dec[c * chunk:c * chunk + 1, ql] + jnp.where(bd, upd, 0.0)
        st_ref[p] = s
        o_parts.append(o_intra + jnp.concatenate(inter, axis=0))
    _gla_finish(jnp.concatenate(o_parts, axis=1), z_ref, gn_ref, y_ref, tb)

    @pl.when(t == pl.num_programs(1) - 1)
    def _():
        for p in range(GLA_H // 2):
            _pairs_to_state(st_ref[p], sout_ref, (0,), p)


def _gla_prompt(qg, kg, vg, zg, la, gn, *, tb):
    b, t, _ = qg.shape
    assert t % tb == 0 and tb % GLA_CHUNK == 0
    tok = lambda w: pl.BlockSpec((1, tb, w), lambda i, j: (i, j, 0))
    return pl.pallas_call(
        functools.partial(_gla_prompt_kernel, tb=tb, chunk=GLA_CHUNK),
        grid=(b, t // tb),
        in_specs=[tok(D_QK), tok(D_QK), tok(D_GLA), tok(D_GLA), tok(D_QK),
                  pl.BlockSpec((1, GLA_DV), lambda i, j: (0, 0))],
        out_specs=[tok(D_GLA), pl.BlockSpec((1, GLA_H, GLA_DK, GLA_DV), lambda i, j: (i, 0, 0, 0))],
        out_shape=[jax.ShapeDtypeStruct((b, t, D_GLA), BF16),
                   jax.ShapeDtypeStruct((b, GLA_H, GLA_DK, GLA_DV), F32)],
        scratch_shapes=[pltpu.VMEM((GLA_H // 2, 2 * GLA_DV, 2 * GLA_DK), F32)],
        compiler_params=_params(("arbitrary", "arbitrary")),
        name="gla_prompt",
    )(qg, kg, vg, zg, la, gn)


def _gla_sample_kernel(q_ref, k_ref, v_ref, z_ref, la_ref, gn_ref, s0_ref, y_ref, sout_ref, *, nb, seq):
    rows = nb * seq
    qt, kt, kp, dec, lower = _gla_prepare(q_ref, k_ref, la_ref, rows, seq)
    v = v_ref[...]
    bd = _pair_blockdiag_mask()
    seq_of_row = lax.shift_right_logical(lax.broadcasted_iota(jnp.int32, (rows, 1), 0), _log2(seq))
    o_parts = []
    for p in range(GLA_H // 2):
        ql = slice(p * 2 * GLA_DK, (p + 1) * 2 * GLA_DK)
        qtp, ktp, kpp = qt[:, ql], kt[:, ql], kp[:, ql]
        vp = v[:, p * 2 * GLA_DV:(p + 1) * 2 * GLA_DV]
        o = _gla_intra(qtp, ktp, vp, lower)
        for i in range(nb):
            mine = seq_of_row == i
            st = _state_to_pairs(s0_ref, (i,), p)
            o = o + jnp.where(mine, _dot_nt(qtp, st.astype(BF16)), 0.0)
            upd = _dot_tn(jnp.where(mine, vp, jnp.zeros_like(vp)), kpp)
            st = st * dec[i * seq:i * seq + 1, ql] + jnp.where(bd, upd, 0.0)
            _pairs_to_state(st, sout_ref, (i,), p)
        o_parts.append(o)
    _gla_finish(jnp.concatenate(o_parts, axis=1), z_ref, gn_ref, y_ref, rows)


def _gla_sample(qg, kg, vg, zg, la, gn, s0, *, seq, nb):
    n = qg.shape[0]
    b = n // seq
    assert b % nb == 0 and (nb * seq) % SUBLANE == 0
    rows = nb * seq
    tok = lambda w: pl.BlockSpec((rows, w), lambda i: (i, 0))
    st = pl.BlockSpec((nb, GLA_H, GLA_DK, GLA_DV), lambda i: (i, 0, 0, 0))
    return pl.pallas_call(
        functools.partial(_gla_sample_kernel, nb=nb, seq=seq),
        grid=(b // nb,),
        in_specs=[tok(D_QK), tok(D_QK), tok(D_GLA), tok(D_GLA), tok(D_QK),
                  pl.BlockSpec((1, GLA_DV), lambda i: (0, 0)), st],
        out_specs=[tok(D_GLA), st],
        out_shape=[jax.ShapeDtypeStruct((n, D_GLA), BF16),
                   jax.ShapeDtypeStruct((b, GLA_H, GLA_DK, GLA_DV), F32)],
        compiler_params=_params(("arbitrary",)),
        name="gla_sample",
    )(qg, kg, vg, zg, la, gn, s0)


def _fox_prompt_kernel(q_ref, k_ref, v_ref, ct_ref, o_ref, *, seq_len, bq):
    lane = lax.broadcasted_iota(jnp.int32, (1, 2 * FOX_HD), 1)
    first = lane < FOX_HD
    row = lax.broadcasted_iota(jnp.int32, (bq, bq), 0)
    col = lax.broadcasted_iota(jnp.int32, (bq, bq), 1)
    causal = col <= row

    def kv_step(j, carry, qs, masked):
        ks = pl.multiple_of(j * bq, bq)
        kb = k_ref[0, :, pl.ds(ks, bq)]
        vb = v_ref[0, :, pl.ds(ks, bq)]
        new = []
        for hl in range(2):
            m, l, acc = carry[hl]
            s = _dot(qs[hl], kb) - ct_ref[0, 0, hl:hl + 1, pl.ds(ks, bq)]
            if masked:
                s = jnp.where(causal, s, -jnp.inf)
            mn = jnp.maximum(m, jnp.max(s, axis=-1, keepdims=True))
            alpha = jnp.exp(m - mn)
            pr = jnp.exp(s - mn)
            l = l * alpha + jnp.sum(pr, axis=-1, keepdims=True)
            acc = acc * alpha + _dot_nt(pr.astype(BF16), vb)
            new.append((mn, l, acc))
        return tuple(new)

    def q_body(i, _):
        q0 = pl.multiple_of(i * bq, bq)
        q = q_ref[0, pl.ds(q0, bq), :]
        zero = jnp.zeros_like(q)
        qs = (jnp.where(first, q, zero), jnp.where(first, zero, q))
        init = tuple((jnp.full((bq, 1), -jnp.inf, F32), jnp.zeros((bq, 1), F32),
                      jnp.zeros((bq, 2 * FOX_HD), F32)) for _ in range(2))
        carry = lax.fori_loop(0, i, lambda j, c: kv_step(j, c, qs, False), init)
        (_, l0, a0), (_, l1, a1) = kv_step(i, carry, qs, True)
        o_ref[0, pl.ds(q0, bq), :] = jnp.where(first, a0 / l0, a1 / l1).astype(BF16)
        return 0

    lax.fori_loop(0, seq_len // bq, q_body, 0)


def _fox_prompt(qf, kft, vft, ct, *, bq):
    b, t, _ = qf.shape
    assert t % bq == 0
    pairs = FOX_H // 2
    tok = pl.BlockSpec((1, t, 2 * FOX_HD), lambda i, p: (i, 0, p))
    feat = pl.BlockSpec((1, 2 * FOX_HD, t), lambda i, p: (i, p, 0))
    return pl.pallas_call(
        functools.partial(_fox_prompt_kernel, seq_len=t, bq=bq),
        grid=(b, pairs),
        in_specs=[tok, feat, feat, pl.BlockSpec((1, 1, 2, t), lambda i, p: (i, p, 0, 0))],
        out_specs=tok,
        out_shape=jax.ShapeDtypeStruct((b, t, D_FOX), BF16),
        compiler_params=_params(("arbitrary", "arbitrary")),
        name="fox_prompt",
    )(qf, kft, vft, ct)


def _lf_prep_kernel(lf_ref, m_ref, o_ref):
    hi, mid, lo = _split3(lf_ref[...])
    o_ref[...] = _dot(hi, m_ref[...]) + _dot(mid, m_ref[...]) + _dot(lo, m_ref[...])


def _lf_prep(lf_rows, *, tp):
    n, page = lf_rows.shape
    src = lax.broadcasted_iota(jnp.int32, (page, 2 * page), 0)
    dst = lax.broadcasted_iota(jnp.int32, (page, 2 * page), 1)
    mask = jnp.where(jnp.logical_or(dst >= page, src > dst), 1.0, 0.0).astype(BF16)
    assert n % tp == 0
    return pl.pallas_call(
        _lf_prep_kernel,
        grid=(n // tp,),
        in_specs=[pl.BlockSpec((tp, page), lambda i: (i, 0)), pl.BlockSpec(mask.shape, lambda i: (0, 0))],
        out_specs=pl.BlockSpec((tp, 2 * page), lambda i: (i, 0)),
        out_shape=jax.ShapeDtypeStruct((n, 2 * page), F32),
        compiler_params=_params(("arbitrary",)),
        name="lf_prep",
    )(lf_rows, mask)


def _fox_sample_kernel(pt_ref, q_ref, kn_ref, vn_ref, cc_ref, *rest, pp, seq, page):
    k_refs, v_refs, e_refs = rest[:pp], rest[pp:2 * pp], rest[2 * pp:3 * pp]
    o_ref, qbd_ref, m_ref, l_ref, acc_ref, car_ref = rest[3 * pp:]
    del pt_ref
    c = pl.program_id(1)
    rows = seq * FOX_H
    sub = lax.broadcasted_iota(jnp.int32, (FOX_H, D_FOX), 0)
    lane = lax.broadcasted_iota(jnp.int32, (FOX_H, D_FOX), 1)
    own = lax.shift_right_logical(lane, _log2(FOX_HD)) == sub

    @pl.when(c == 0)
    def _():
        q = q_ref[0]
        qbd_ref[...] = jnp.concatenate([jnp.where(own, q[i:i + 1, :], 0.0) for i in range(seq)], axis=0)
        m_ref[...] = jnp.full_like(m_ref, -jnp.inf)
        l_ref[...] = jnp.zeros_like(l_ref)
        acc_ref[...] = jnp.zeros_like(acc_ref)
        car_ref[...] = jnp.zeros_like(car_ref)

    qbd = qbd_ref[...]
    carry = car_ref[...]
    bias = [None] * pp
    for i in reversed(range(pp)):
        e = e_refs[i][0]
        bias[i] = jnp.concatenate([e[:, 0:page] + carry] * seq, axis=0)
        carry = carry + e[:, page:2 * page]
    car_ref[...] = carry
    kcat = jnp.concatenate([k_refs[i][0] for i in range(pp)], axis=1)
    s = _dot(qbd, kcat) + jnp.concatenate(bias, axis=1)
    m_old = m_ref[...]
    mn = jnp.maximum(m_old, jnp.max(s, axis=-1, keepdims=True))
    alpha = jnp.exp(m_old - mn)
    pr = jnp.exp(s - mn)
    l_ref[...] = l_ref[...] * alpha + jnp.sum(pr, axis=-1, keepdims=True)
    vcat = jnp.concatenate([v_refs[i][0] for i in range(pp)], axis=1)
    acc_ref[...] = acc_ref[...] * alpha + _dot_nt(pr, vcat)
    m_ref[...] = mn

    @pl.when(c == pl.num_programs(1) - 1)
    def _():
        kn = kn_ref[0]
        vn = vn_ref[0]
        cc = cc_ref[0]
        eye = (lax.broadcasted_iota(jnp.int32, (FOX_H, FOX_H), 0)
               == lax.broadcasted_iota(jnp.int32, (FOX_H, FOX_H), 1))
        q_of_row = lax.shift_right_logical(lax.broadcasted_iota(jnp.int32, (rows, 1), 0), _log2(FOX_H))
        s_new = []
        for j in range(seq):
            cj = jnp.sum(jnp.where(eye, cc[j:j + 1, :], 0.0), axis=-1, keepdims=True)
            sj = jnp.sum(qbd * kn[j:j + 1, :], axis=-1, keepdims=True) - jnp.concatenate([cj] * seq, axis=0)
            s_new.append(jnp.where(q_of_row >= j, sj, -jnp.inf))
        m_prev = m_ref[...]
        m_fin = m_prev
        for sj in s_new:
            m_fin = jnp.maximum(m_fin, sj)
        a_fin = jnp.exp(m_prev - m_fin)
        l_fin = l_ref[...] * a_fin
        acc = acc_ref[...] * a_fin
        for j, sj in enumerate(s_new):
            pj = jnp.exp(sj - m_fin)
            l_fin = l_fin + pj
            acc = acc + pj * vn[j:j + 1, :]
        out = acc / l_fin
        o_ref[0] = jnp.concatenate(
            [jnp.sum(jnp.where(own, out[i * FOX_H:(i + 1) * FOX_H, :], 0.0), axis=0, keepdims=True)
             for i in range(seq)], axis=0)


def _fox_sample(page_table, qf, kn, vn, cc, cache_kt, cache_vt, e_pages, *, pp):
    b, seq, _ = qf.shape
    n_pages = page_table.shape[1]
    assert n_pages % pp == 0
    page = cache_kt.shape[2]
    n_chunks = n_pages // pp
    tok = lambda w: pl.BlockSpec((1, seq, w), lambda i, c, pt: (i, 0, 0))

    def paged(shape, k):
        return pl.BlockSpec((1,) + shape, lambda i, c, pt: (pt[i, (n_chunks - 1 - c) * pp + k], 0, 0))

    rows = seq * FOX_H
    grid_spec = pltpu.PrefetchScalarGridSpec(
        num_scalar_prefetch=1,
        grid=(b, n_chunks),
        in_specs=[tok(D_FOX), tok(D_FOX), tok(D_FOX), tok(FOX_H)]
        + [paged((D_FOX, page), k) for k in range(pp)]
        + [paged((D_FOX, page), k) for k in range(pp)]
        + [paged((FOX_H, 2 * page), k) for k in range(pp)],
        out_specs=tok(D_FOX),
        scratch_shapes=[pltpu.VMEM((rows, D_FOX), F32), pltpu.VMEM((rows, 1), F32), pltpu.VMEM((rows, 1), F32),
                        pltpu.VMEM((rows, D_FOX), F32), pltpu.VMEM((FOX_H, page), F32)],
    )
    return pl.pallas_call(
        functools.partial(_fox_sample_kernel, pp=pp, seq=seq, page=page),
        grid_spec=grid_spec,
        out_shape=jax.ShapeDtypeStruct((b, seq, D_FOX), F32),
        compiler_params=_params(("arbitrary", "arbitrary")),
        name="fox_sample",
    )(page_table, qf, kn, vn, cc, *([cache_kt] * pp), *([cache_vt] * pp), *([e_pages] * pp))


def _out_kernel(x_ref, yg_ref, of_ref, zf_ref, w_ref, gf_ref, y_ref):
    yf = (of_ref[...].astype(F32) * _silu(zf_ref[...].astype(F32))).astype(BF16)
    d = _dot(yg_ref[...], w_ref[0:D_GLA, :]) + _dot(yf, w_ref[D_GLA:D_GLA + D_FOX, :])
    xo = x_ref[...] + d
    ms = jnp.mean(xo * xo, axis=-1, keepdims=True)
    y_ref[...] = xo * lax.rsqrt(ms + RMS_EPS) * gf_ref[...]


def _out(x2d, yg, of, zf, w_out, g_final, *, tm):
    n, d = x2d.shape
    rows = lambda w: pl.BlockSpec((tm, w), lambda i: (i, 0))
    full = lambda a: pl.BlockSpec(a.shape, lambda i: (0, 0))
    return pl.pallas_call(
        _out_kernel,
        grid=(n // tm,),
        in_specs=[rows(d), rows(D_GLA), rows(D_FOX), rows(D_FOX), full(w_out), full(g_final)],
        out_specs=rows(d),
        out_shape=jax.ShapeDtypeStruct((n, d), F32),
        compiler_params=_params(("arbitrary",)),
        name="out",
    )(x2d, yg, of, zf, w_out, g_final)


def _regroup_weights(w_in, w_a2, b_a, b_f):
    o_alr = 2 * D_QK + 2 * D_GLA
    o_qf = o_alr + GLA_RANK
    o_fl = o_qf + 4 * D_FOX
    d = w_in.shape[0]
    w_main = jnp.concatenate([w_in[:, :o_alr], w_in[:, o_qf:o_fl]], axis=1).astype(BF16)
    w_kvt = w_in[:, o_qf + D_FOX:o_qf + 3 * D_FOX].T.astype(BF16)
    pad = LANE - FOX_H - GLA_RANK
    w_small = jnp.concatenate([w_in[:, o_fl:o_fl + FOX_H], w_in[:, o_alr:o_qf], jnp.zeros((d, pad), F32)],
                              axis=1).astype(BF16)
    w_a2p = jnp.concatenate([jnp.zeros((FOX_H, D_QK), F32), w_a2, jnp.zeros((pad, D_QK), F32)],
                            axis=0).astype(BF16)
    b_fp = jnp.concatenate([b_f, jnp.zeros((LANE - FOX_H,), F32)])[None, :]
    return w_main, w_kvt, w_small, w_a2p, b_a[None, :], b_fp


def kernel(x_prompt, x_sample, cache_k, cache_v, cache_logf, state_gla, page_table,
           g_norm, w_in, w_a2, b_a, b_f, gla_norm, w_out, g_final):
    depth = w_in.shape[0]
    assert depth == 1, "single-layer step"
    bp, tp, d = x_prompt.shape
    bs, ts, _ = x_sample.shape
    n_pool, page = cache_k.shape[1], cache_k.shape[2]
    l = 0
    weights = _regroup_weights(w_in[l], w_a2[l], b_a[l], b_f[l])
    g = g_norm[l][None, :]
    gn = gla_norm[l][None, :]
    w_o = w_out[l].astype(BF16)
    gf = g_final[None, :]

    xp = x_prompt.reshape(bp * tp, d)
    pr = _proj(xp, g, *weights, seq_len=tp, tm=min(512, tp))
    r3 = lambda a: a.reshape(bp, tp, a.shape[-1])
    yg, s_prompt = _gla_prompt(r3(pr["qg"]), r3(pr["kg"]), r3(pr["vg"]), r3(pr["zg"]), r3(pr["la"]), gn,
                               tb=min(256, tp))
    of = _fox_prompt(r3(pr["qf"]), pr["kfbt"], pr["vfbt"], pr["ct"].reshape(bp, FOX_H // 2, 2, tp),
                     bq=min(256, tp))
    y_prompt = _out(xp, yg.reshape(bp * tp, D_GLA), of.reshape(bp * tp, D_FOX), pr["zf"], w_o, gf,
                    tm=min(512, bp * tp))
    heads_last = lambda a: a.reshape(bp, FOX_H, FOX_HD, tp).transpose(0, 3, 1, 2)[None]
    k_prompt, v_prompt = heads_last(pr["kft"]), heads_last(pr["vft"])
    lf_prompt = pr["lft"].transpose(0, 2, 1)[None]

    xs = x_sample.reshape(bs * ts, d)
    sm = _proj(xs, g, *weights, seq_len=ts, tm=min(512, bs * ts))
    yg_s, s_sample = _gla_sample(sm["qg"], sm["kg"], sm["vg"], sm["zg"], sm["la"], gn, state_gla[l], seq=ts, nb=8)
    ckt = cache_k[l].transpose(0, 2, 3, 1).reshape(n_pool, D_FOX, page)
    cvt = cache_v[l].transpose(0, 2, 3, 1).reshape(n_pool, D_FOX, page)
    clf = cache_logf[l].transpose(0, 2, 1).reshape(n_pool * FOX_H, page)
    e_pages = _lf_prep(clf, tp=math.gcd(n_pool * FOX_H, 2048)).reshape(n_pool, FOX_H, 2 * page)
    s3 = lambda a: a.reshape(bs, ts, a.shape[-1])
    of_s = _fox_sample(page_table, s3(sm["qf"].astype(F32)), s3(sm["kf"]), s3(sm["vf"]), s3(sm["cc"]),
                       ckt, cvt, e_pages, pp=8)
    y_sample = _out(xs, yg_s, of_s.reshape(bs * ts, D_FOX), sm["zf"], w_o, gf, tm=min(512, bs * ts))

    return (y_prompt.reshape(bp, tp, d), y_sample.reshape(bs, ts, d),
            k_prompt, v_prompt, lf_prompt, s_prompt[None],
            sm["kf"].reshape(1, bs, ts, FOX_H, FOX_HD), sm["vf"].reshape(1, bs, ts, FOX_H, FOX_HD),
            sm["lf"].reshape(1, bs, ts, FOX_H), s_sample[None])
```

```python
import functools
import math

import jax
import jax.numpy as jnp
from jax import lax
from jax.experimental import pallas as pl
from jax.experimental.pallas import tpu as pltpu

F32 = jnp.float32
BF16 = jnp.bfloat16

GLA_H = 4
GLA_DK = 64
GLA_DV = 128
GLA_RANK = 16
GLA_GATE_NORM = 16.0
GLA_CHUNK = 32
FOX_H = 8
FOX_HD = 64
RMS_EPS = 1e-6
_LOG2E = 1.4426950408889634

D_QK = GLA_H * GLA_DK
D_GLA = GLA_H * GLA_DV
D_FOX = FOX_H * FOX_HD
LANE = 128
SUBLANE = 8
VMEM_LIMIT = 56 * 1024 * 1024

_OFF_QG, _OFF_KG, _OFF_VG, _OFF_ZG = 0, 256, 512, 1024
_OFF_QF, _OFF_KF, _OFF_VF, _OFF_ZF, _OFF_END = 1536, 2048, 2560, 3072, 3584


def _dot(a, b):
    return jnp.dot(a, b, preferred_element_type=F32)


def _dot_nt(a, b):
    return lax.dot_general(a, b, (((1,), (1,)), ((), ())), preferred_element_type=F32)


def _dot_tn(a, b):
    return lax.dot_general(a, b, (((0,), (0,)), ((), ())), preferred_element_type=F32)


def _split3(x):
    hi = x.astype(BF16)
    r = x - hi.astype(F32)
    mid = r.astype(BF16)
    lo = (r - mid.astype(F32)).astype(BF16)
    return hi, mid, lo


def _dot_exact_mask(mask01, x):
    hi, mid, lo = _split3(x)
    return _dot(mask01, hi) + _dot(mask01, mid) + _dot(mask01, lo)


def _log_sigmoid(x):
    return jnp.minimum(x, 0.0) - jnp.log1p(jnp.exp(-jnp.abs(x)))


def _silu(x):
    return x / (1.0 + jnp.exp(-x))


def _log2(n):
    k = n.bit_length() - 1
    assert (1 << k) == n, f"{n} must be a power of two"
    return k


def _params(sem):
    return pltpu.CompilerParams(dimension_semantics=sem, vmem_limit_bytes=VMEM_LIMIT)


def _proj_kernel(x_ref, g_ref, wm_ref, wkvt_ref, ws_ref, wa_ref, ba_ref, bf_ref, *refs, tm, seq_len, names):
    out = dict(zip(names, refs))
    carry_ref = refs[len(names)]
    x = x_ref[...]
    ms = jnp.mean(x * x, axis=-1, keepdims=True)
    h = (x * lax.rsqrt(ms + RMS_EPS) * g_ref[...]).astype(BF16)

    def grp(a, b):
        return _dot(h, wm_ref[:, a:b])

    out["qg"][...] = (grp(_OFF_QG, _OFF_KG) * (GLA_DK ** -0.5)).astype(BF16)
    out["kg"][...] = grp(_OFF_KG, _OFF_VG).astype(BF16)
    out["vg"][...] = grp(_OFF_VG, _OFF_ZG).astype(BF16)
    out["zg"][...] = grp(_OFF_ZG, _OFF_QF).astype(BF16)
    out["zf"][...] = grp(_OFF_ZF, _OFF_END).astype(BF16)

    small = _dot(h, ws_ref[...])
    la = _log_sigmoid(_dot(small.astype(BF16), wa_ref[...]) + ba_ref[...])
    out["la"][...] = la * (1.0 / GLA_GATE_NORM)
    lf = _log_sigmoid(small + bf_ref[...])

    row = lax.broadcasted_iota(jnp.int32, (tm, tm), 0)
    col = lax.broadcasted_iota(jnp.int32, (tm, tm), 1)
    keep = col <= row
    if seq_len < tm:
        sh = _log2(seq_len)
        keep = jnp.logical_and(keep, lax.shift_right_logical(row, sh) == lax.shift_right_logical(col, sh))
    tri = jnp.where(keep, 1.0, 0.0).astype(BF16)
    c = _dot_exact_mask(tri, lf)

    if seq_len >= tm:
        steps = seq_len // tm
        if steps > 1:
            @pl.when(pl.program_id(0) % steps == 0)
            def _():
                carry_ref[...] = jnp.zeros_like(carry_ref)

            c = c + carry_ref[0:1, :]
            carry_ref[0:1, :] = c[tm - 1:tm, :]
        qkvt = _dot_nt(wkvt_ref[...], h)
        out["qft"][0] = (qkvt[0:D_FOX] * (FOX_HD ** -0.5 * _LOG2E)).astype(BF16)
        out["kft"][0] = qkvt[D_FOX:2 * D_FOX]
        out["vft"][0] = qkvt[2 * D_FOX:3 * D_FOX]
        out["vfbt"][0] = qkvt[2 * D_FOX:3 * D_FOX].astype(BF16)
        out["kfb"][...] = grp(_OFF_KF, _OFF_VF).astype(BF16)
        out["lft"][0] = lf.T[0:FOX_H, :]
        out["ct"][0] = c.T[0:FOX_H, :]
    else:
        out["qf"][...] = (grp(_OFF_QF, _OFF_KF) * (FOX_HD ** -0.5)).astype(BF16)
        out["kf"][...] = grp(_OFF_KF, _OFF_VF)
        out["vf"][...] = grp(_OFF_VF, _OFF_ZF)
        out["lf"][...] = lf[:, 0:FOX_H]
        out["cc"][...] = c[:, 0:FOX_H]


def _proj(x2d, g, w_main, w_kvt, w_small, w_a2p, b_a, b_fp, *, seq_len, tm):
    n, d = x2d.shape
    assert n % tm == 0 and (seq_len % tm == 0 or tm % seq_len == 0)
    rows = lambda w: pl.BlockSpec((tm, w), lambda i: (i, 0))
    full = lambda a: pl.BlockSpec(a.shape, lambda i: (0, 0))
    tok = lambda w, dt: (jax.ShapeDtypeStruct((n, w), dt), rows(w))
    outs = {"qg": tok(D_QK, BF16), "kg": tok(D_QK, BF16), "vg": tok(D_GLA, BF16), "zg": tok(D_GLA, BF16),
            "la": tok(D_QK, F32), "zf": tok(D_FOX, BF16)}
    if seq_len >= tm:
        steps = seq_len // tm
        nseq = n // seq_len
        feat = lambda w, dt: (jax.ShapeDtypeStruct((nseq, w, seq_len), dt),
                              pl.BlockSpec((1, w, tm), lambda i: (i // steps, 0, i % steps)))
        outs.update({"qft": feat(D_FOX, BF16), "kfb": tok(D_FOX, BF16), "kft": feat(D_FOX, F32),
                     "vft": feat(D_FOX, F32), "vfbt": feat(D_FOX, BF16),
                     "lft": feat(FOX_H, F32), "ct": feat(FOX_H, F32)})
    else:
        outs.update({"qf": tok(D_FOX, BF16), "kf": tok(D_FOX, F32), "vf": tok(D_FOX, F32),
                     "lf": tok(FOX_H, F32), "cc": tok(FOX_H, F32)})
    names = tuple(outs)
    res = pl.pallas_call(
        functools.partial(_proj_kernel, tm=tm, seq_len=seq_len, names=names),
        grid=(n // tm,),
        in_specs=[rows(d), full(g), full(w_main), full(w_kvt), full(w_small), full(w_a2p), full(b_a), full(b_fp)],
        out_specs=[outs[k][1] for k in names],
        out_shape=[outs[k][0] for k in names],
        scratch_shapes=[pltpu.VMEM((SUBLANE, LANE), F32)],
        compiler_params=_params(("arbitrary",)),
        name="proj",
    )(x2d, g, w_main, w_kvt, w_small, w_a2p, b_a, b_fp)
    return dict(zip(names, res))


def _gla_prepare(q_ref, k_ref, la_ref, rows, chunk):
    la = la_ref[...].reshape(rows, D_QK)
    sh = _log2(chunk)
    row = lax.broadcasted_iota(jnp.int32, (rows, rows), 0)
    col = lax.broadcasted_iota(jnp.int32, (rows, rows), 1)
    same = lax.shift_right_logical(row, sh) == lax.shift_right_logical(col, sh)
    lower = jnp.logical_and(same, col <= row)
    upper = jnp.logical_and(same, col > row)
    cum = _dot_exact_mask(jnp.where(lower, 1.0, 0.0).astype(BF16), la)
    suf = _dot_exact_mask(jnp.where(upper, 1.0, 0.0).astype(BF16), la)
    q = q_ref[...].reshape(rows, D_QK).astype(F32)
    k = k_ref[...].reshape(rows, D_QK).astype(F32)
    qt = (q * jnp.exp(cum)).astype(BF16)
    kt = (k * jnp.exp(-cum)).astype(BF16)
    kp = (k * jnp.exp(suf)).astype(BF16)
    dec = jnp.exp(cum + suf)
    return qt, kt, kp, dec, lower


def _gla_intra(qtp, ktp, vp, lower):
    lane = lax.broadcasted_iota(jnp.int32, (1, 2 * GLA_DK), 1)
    first = lane < GLA_DK
    zero = jnp.zeros_like(qtp)
    a0 = jnp.where(lower, _dot_nt(jnp.where(first, qtp, zero), ktp), 0.0).astype(BF16)
    a1 = jnp.where(lower, _dot_nt(jnp.where(first, zero, qtp), ktp), 0.0).astype(BF16)
    vlane = lax.broadcasted_iota(jnp.int32, (1, 2 * GLA_DV), 1)
    vzero = jnp.zeros_like(vp)
    v0 = jnp.where(vlane < GLA_DV, vp, vzero)
    v1 = jnp.where(vlane < GLA_DV, vzero, vp)
    return _dot(a0, v0) + _dot(a1, v1)


def _pair_blockdiag_mask():
    r = lax.broadcasted_iota(jnp.int32, (2 * GLA_DV, 2 * GLA_DK), 0)
    c = lax.broadcasted_iota(jnp.int32, (2 * GLA_DV, 2 * GLA_DK), 1)
    return lax.shift_right_logical(r, _log2(GLA_DV)) == lax.shift_right_logical(c, _log2(GLA_DK))


def _gla_finish(o, z_ref, gn_ref, y_ref, rows):
    ys = []
    for h in range(GLA_H):
        oh = o[:, h * GLA_DV:(h + 1) * GLA_DV]
        ms = jnp.mean(oh * oh, axis=-1, keepdims=True)
        ys.append(oh * lax.rsqrt(ms + RMS_EPS) * gn_ref[...])
    y = jnp.concatenate(ys, axis=1) * _silu(z_ref[...].reshape(rows, D_GLA).astype(F32))
    y_ref[...] = y.astype(BF16).reshape(y_ref.shape)


def _state_to_pairs(s_ref, idx, p):
    sa = s_ref[idx + (2 * p,)]
    sb = s_ref[idx + (2 * p + 1,)]
    z = jnp.zeros_like(sa)
    sbd = jnp.concatenate([jnp.concatenate([sa, z], axis=1), jnp.concatenate([z, sb], axis=1)], axis=0)
    return sbd.T


def _pairs_to_state(st, s_ref, idx, p):
    sbd = st.T
    s_ref[idx + (2 * p,)] = sbd[0:GLA_DK, 0:GLA_DV]
    s_ref[idx + (2 * p + 1,)] = sbd[GLA_DK:2 * GLA_DK, GLA_DV:2 * GLA_DV]


def _gla_prompt_kernel(q_ref, k_ref, v_ref, z_ref, la_ref, gn_ref, y_ref, sout_ref, st_ref, *, tb, chunk):
    t = pl.program_id(1)

    @pl.when(t == 0)
    def _():
        st_ref[...] = jnp.zeros_like(st_ref)

    qt, kt, kp, dec, lower = _gla_prepare(q_ref, k_ref, la_ref, tb, chunk)
    v = v_ref[0]
    bd = _pair_blockdiag_mask()
    o_parts = []
    for p in range(GLA_H // 2):
        ql = slice(p * 2 * GLA_DK, (p + 1) * 2 * GLA_DK)
        qtp, ktp, kpp = qt[:, ql], kt[:, ql], kp[:, ql]
        vp = v[:, p * 2 * GLA_DV:(p + 1) * 2 * GLA_DV]
        o_intra = _gla_intra(qtp, ktp, vp, lower)
        s = st_ref[p]
        inter = []
        for c in range(tb // chunk):
            sl = slice(c * chunk, (c + 1) * chunk)
            inter.append(_dot_nt(qtp[sl], s.astype(BF16)))
            upd = _dot_tn(vp[sl], kpp[sl])
            s = s * dec[c * chunk:c * chunk + 1, ql] + jnp.where(bd, upd, 0.0)
        st_ref[p] = s
        o_parts.append(o_intra + jnp.concatenate(inter, axis=0))
    _gla_finish(jnp.concatenate(o_parts, axis=1), z_ref, gn_ref, y_ref, tb)

    @pl.when(t == pl.num_programs(1) - 1)
    def _():
        for p in range(GLA_H // 2):
            _pairs_to_state(st_ref[p], sout_ref, (0,), p)


def _gla_prompt(qg, kg, vg, zg, la, gn, *, tb):
    b, t, _ = qg.shape
    assert t % tb == 0 and tb % GLA_CHUNK == 0
    tok = lambda w: pl.BlockSpec((1, tb, w), lambda i, j: (i, j, 0))
    return pl.pallas_call(
        functools.partial(_gla_prompt_kernel, tb=tb, chunk=GLA_CHUNK),
        grid=(b, t // tb),
        in_specs=[tok(D_QK), tok(D_QK), tok(D_GLA), tok(D_GLA), tok(D_QK),
                  pl.BlockSpec((1, GLA_DV), lambda i, j: (0, 0))],
        out_specs=[tok(D_GLA), pl.BlockSpec((1, GLA_H, GLA_DK, GLA_DV), lambda i, j: (i, 0, 0, 0))],
        out_shape=[jax.ShapeDtypeStruct((b, t, D_GLA), BF16),
                   jax.ShapeDtypeStruct((b, GLA_H, GLA_DK, GLA_DV), F32)],
        scratch_shapes=[pltpu.VMEM((GLA_H // 2, 2 * GLA_DV, 2 * GLA_DK), F32)],
        compiler_params=_params(("arbitrary", "arbitrary")),
        name="gla_prompt",
    )(qg, kg, vg, zg, la, gn)


def _gla_sample_kernel(q_ref, k_ref, v_ref, z_ref, la_ref, gn_ref, s0_ref, y_ref, sout_ref, *, nb, seq):
    rows = nb * seq
    qt, kt, kp, dec, lower = _gla_prepare(q_ref, k_ref, la_ref, rows, seq)
    v = v_ref[...]
    bd = _pair_blockdiag_mask()
    seq_of_row = lax.shift_right_logical(lax.broadcasted_iota(jnp.int32, (rows, 1), 0), _log2(seq))
    o_parts = []
    for p in range(GLA_H // 2):
        ql = slice(p * 2 * GLA_DK, (p + 1) * 2 * GLA_DK)
        qtp, ktp, kpp = qt[:, ql], kt[:, ql], kp[:, ql]
        vp = v[:, p * 2 * GLA_DV:(p + 1) * 2 * GLA_DV]
        o = _gla_intra(qtp, ktp, vp, lower)
        for i in range(nb):
            mine = seq_of_row == i
            st = _state_to_pairs(s0_ref, (i,), p)
            o = o + jnp.where(mine, _dot_nt(qtp, st.astype(BF16)), 0.0)
            upd = _dot_tn(jnp.where(mine, vp, jnp.zeros_like(vp)), kpp)
            st = st * dec[i * seq:i * seq + 1, ql] + jnp.where(bd, upd, 0.0)
            _pairs_to_state(st, sout_ref, (i,), p)
        o_parts.append(o)
    _gla_finish(jnp.concatenate(o_parts, axis=1), z_ref, gn_ref, y_ref, rows)


def _gla_sample(qg, kg, vg, zg, la, gn, s0, *, seq, nb):
    n = qg.shape[0]
    b = n // seq
    assert b % nb == 0 and (nb * seq) % SUBLANE == 0
    rows = nb * seq
    tok = lambda w: pl.BlockSpec((rows, w), lambda i: (i, 0))
    st = pl.BlockSpec((nb, GLA_H, GLA_DK, GLA_DV), lambda i: (i, 0, 0, 0))
    return pl.pallas_call(
        functools.partial(_gla_sample_kernel, nb=nb, seq=seq),
        grid=(b // nb,),
        in_specs=[tok(D_QK), tok(D_QK), tok(D_GLA), tok(D_GLA), tok(D_QK),
                  pl.BlockSpec((1, GLA_DV), lambda i: (0, 0)), st],
        out_specs=[tok(D_GLA), st],
        out_shape=[jax.ShapeDtypeStruct((n, D_GLA), BF16),
                   jax.ShapeDtypeStruct((b, GLA_H, GLA_DK, GLA_DV), F32)],
        compiler_params=_params(("arbitrary",)),
        name="gla_sample",
    )(qg, kg, vg, zg, la, gn, s0)


def _fox_prompt_kernel(q_ref, k_ref, v_ref, ct_ref, o_ref, kaug_ref, qs_ref, s_ref, m_ref, l_ref, acc_ref,
                       *, seq_len, bq, hp):
    pw = 2 * FOX_HD
    n_split = 3
    tc = min(512, seq_len)
    sub = lax.broadcasted_iota(jnp.int32, (pw, 1), 0)
    first = sub < FOX_HD
    key = lax.broadcasted_iota(jnp.int32, (bq, bq), 0)
    qry = lax.broadcasted_iota(jnp.int32, (bq, bq), 1)
    causal = key <= qry
    sel = [jnp.broadcast_to(jnp.where(jnp.logical_and(sub >= n_split * hl, sub < n_split * (hl + 1)),
                                      1.0, 0.0).astype(BF16), (pw, bq)) for hl in range(2)]

    for pi in range(hp):
        kaug_ref[pi, :, 0:pw] = k_ref[0, :, pi * pw:(pi + 1) * pw]
        parts = []
        for hl in range(2):
            parts.extend(p.astype(F32) for p in _split3(ct_ref[0, pi, hl:hl + 1, :] * (-_LOG2E)))
        parts.append(jnp.zeros((pw - 2 * n_split, seq_len), F32))
        bias_rows = jnp.concatenate(parts, axis=0)
        for t0 in range(0, seq_len, tc):
            kaug_ref[pi, t0:t0 + tc, pw:2 * pw] = bias_rows[:, t0:t0 + tc].T.astype(BF16)

    def scores(j, slot):
        ks = pl.multiple_of(j * bq, bq)
        for h in range(2 * hp):
            s_ref[slot, h] = _dot(kaug_ref[h // 2, pl.ds(ks, bq), :], qs_ref[h])

    def consume(j, slot, masked):
        ks = pl.multiple_of(j * bq, bq)
        for h in range(2 * hp):
            s = s_ref[slot, h]
            if masked:
                s = jnp.where(causal, s, -jnp.inf)
            m = m_ref[h]
            mn = jnp.maximum(m, jnp.max(s, axis=0, keepdims=True))
            alpha = jnp.exp2(m - mn)
            pr = jnp.exp2(s - mn)
            m_ref[h] = mn
            l_ref[h] = l_ref[h] * alpha + jnp.sum(pr, axis=0, keepdims=True)
            vb = v_ref[0, h * FOX_HD:(h + 1) * FOX_HD, pl.ds(ks, bq)]
            acc_ref[h] = acc_ref[h] * alpha + _dot(vb, pr.astype(BF16))

    def kv_pair(jj, _):
        j = 2 * jj
        scores(j + 1, 1)
        consume(j, 0, False)
        scores(j + 2, 0)
        consume(j + 1, 1, False)
        return 0

    def q_body(i, _):
        q0 = pl.multiple_of(i * bq, bq)
        for pi in range(hp):
            qt = q_ref[0, pi * pw:(pi + 1) * pw, pl.ds(q0, bq)]
            zero = jnp.zeros_like(qt)
            qs_ref[2 * pi] = jnp.concatenate([jnp.where(first, qt, zero), sel[0]], axis=0)
            qs_ref[2 * pi + 1] = jnp.concatenate([jnp.where(first, zero, qt), sel[1]], axis=0)
        m_ref[...] = jnp.full_like(m_ref, -jnp.inf)
        l_ref[...] = jnp.zeros_like(l_ref)
        acc_ref[...] = jnp.zeros_like(acc_ref)
        scores(0, 0)
        lax.fori_loop(0, i // 2, kv_pair, 0)
        odd = i % 2 == 1

        @pl.when(odd)
        def _():
            scores(i, 1)
            consume(i - 1, 0, False)
            consume(i, 1, True)

        @pl.when(jnp.logical_not(odd))
        def _():
            consume(i, 0, True)

        for pi in range(hp):
            pair_t = jnp.concatenate([acc_ref[2 * pi] / l_ref[2 * pi],
                                      acc_ref[2 * pi + 1] / l_ref[2 * pi + 1]], axis=0)
            o_ref[0, pl.ds(q0, bq), pi * pw:(pi + 1) * pw] = pair_t.T.astype(BF16)
        return 0

    lax.fori_loop(0, seq_len // bq, q_body, 0)


def _fox_prompt(qft, kf, vft, ct, *, bq, hp):
    b, _, t = qft.shape
    pairs = FOX_H // 2
    assert t % bq == 0 and pairs % hp == 0
    pw = 2 * FOX_HD
    tok = pl.BlockSpec((1, t, hp * pw), lambda i, p: (i, 0, p))
    feat = pl.BlockSpec((1, hp * pw, t), lambda i, p: (i, p, 0))
    return pl.pallas_call(
        functools.partial(_fox_prompt_kernel, seq_len=t, bq=bq, hp=hp),
        grid=(b, pairs // hp),
        in_specs=[feat, tok, feat, pl.BlockSpec((1, hp, 2, t), lambda i, p: (i, p, 0, 0))],
        out_specs=tok,
        out_shape=jax.ShapeDtypeStruct((b, t, D_FOX), BF16),
        scratch_shapes=[pltpu.VMEM((hp, t, 2 * pw), BF16),
                        pltpu.VMEM((2 * hp, 2 * pw, bq), BF16),
                        pltpu.VMEM((2, 2 * hp, bq, bq), F32),
                        pltpu.VMEM((2 * hp, 1, bq), F32),
                        pltpu.VMEM((2 * hp, 1, bq), F32),
                        pltpu.VMEM((2 * hp, FOX_HD, bq), F32)],
        compiler_params=_params(("arbitrary", "arbitrary")),
        name="fox_prompt",
    )(qft, kf, vft, ct)


def _lf_prep_kernel(lf_ref, m_ref, o_ref):
    hi, mid, lo = _split3(lf_ref[...])
    o_ref[...] = _dot(hi, m_ref[...]) + _dot(mid, m_ref[...]) + _dot(lo, m_ref[...])


def _lf_prep(lf_rows, *, tp):
    n, page = lf_rows.shape
    src = lax.broadcasted_iota(jnp.int32, (page, 2 * page), 0)
    dst = lax.broadcasted_iota(jnp.int32, (page, 2 * page), 1)
    mask = jnp.where(jnp.logical_or(dst >= page, src > dst), 1.0, 0.0).astype(BF16)
    assert n % tp == 0
    return pl.pallas_call(
        _lf_prep_kernel,
        grid=(n // tp,),
        in_specs=[pl.BlockSpec((tp, page), lambda i: (i, 0)), pl.BlockSpec(mask.shape, lambda i: (0, 0))],
        out_specs=pl.BlockSpec((tp, 2 * page), lambda i: (i, 0)),
        out_shape=jax.ShapeDtypeStruct((n, 2 * page), F32),
        compiler_params=_params(("arbitrary",)),
        name="lf_prep",
    )(lf_rows, mask)


def _fox_sample_kernel(pt_ref, q_ref, kn_ref, vn_ref, cc_ref, *rest, pp, seq, page):
    k_refs, v_refs, e_refs = rest[:pp], rest[pp:2 * pp], rest[2 * pp:3 * pp]
    o_ref, qbd_ref, m_ref, l_ref, acc_ref, car_ref = rest[3 * pp:]
    del pt_ref
    c = pl.program_id(1)
    rows = seq * FOX_H
    sub = lax.broadcasted_iota(jnp.int32, (FOX_H, D_FOX), 0)
    lane = lax.broadcasted_iota(jnp.int32, (FOX_H, D_FOX), 1)
    own = lax.shift_right_logical(lane, _log2(FOX_HD)) == sub

    @pl.when(c == 0)
    def _():
        q = q_ref[0]
        qbd_ref[...] = jnp.concatenate([jnp.where(own, q[i:i + 1, :], 0.0) for i in range(seq)], axis=0)
        m_ref[...] = jnp.full_like(m_ref, -jnp.inf)
        l_ref[...] = jnp.zeros_like(l_ref)
        acc_ref[...] = jnp.zeros_like(acc_ref)
        car_ref[...] = jnp.zeros_like(car_ref)

    qbd = qbd_ref[...]
    carry = car_ref[...]
    bias = [None] * pp
    for i in reversed(range(pp)):
        e = e_refs[i][0]
        bias[i] = jnp.concatenate([e[:, 0:page] + carry] * seq, axis=0)
        carry = carry + e[:, page:2 * page]
    car_ref[...] = carry
    kcat = jnp.concatenate([k_refs[i][0] for i in range(pp)], axis=1)
    s = _dot(qbd, kcat) + jnp.concatenate(bias, axis=1)
    m_old = m_ref[...]
    mn = jnp.maximum(m_old, jnp.max(s, axis=-1, keepdims=True))
    alpha = jnp.exp(m_old - mn)
    pr = jnp.exp(s - mn)
    l_ref[...] = l_ref[...] * alpha + jnp.sum(pr, axis=-1, keepdims=True)
    vcat = jnp.concatenate([v_refs[i][0] for i in range(pp)], axis=1)
    acc_ref[...] = acc_ref[...] * alpha + _dot_nt(pr, vcat)
    m_ref[...] = mn

    @pl.when(c == pl.num_programs(1) - 1)
    def _():
        kn = kn_ref[0]
        vn = vn_ref[0]
        cc = cc_ref[0]
        eye = (lax.broadcasted_iota(jnp.int32, (FOX_H, FOX_H), 0)
               == lax.broadcasted_iota(jnp.int32, (FOX_H, FOX_H), 1))
        q_of_row = lax.shift_right_logical(lax.broadcasted_iota(jnp.int32, (rows, 1), 0), _log2(FOX_H))
        s_new = []
        for j in range(seq):
            cj = jnp.sum(jnp.where(eye, cc[j:j + 1, :], 0.0), axis=-1, keepdims=True)
            sj = jnp.sum(qbd * kn[j:j + 1, :], axis=-1, keepdims=True) - jnp.concatenate([cj] * seq, axis=0)
            s_new.append(jnp.where(q_of_row >= j, sj, -jnp.inf))
        m_prev = m_ref[...]
        m_fin = m_prev
        for sj in s_new:
            m_fin = jnp.maximum(m_fin, sj)
        a_fin = jnp.exp(m_prev - m_fin)
        l_fin = l_ref[...] * a_fin
        acc = acc_ref[...] * a_fin
        for j, sj in enumerate(s_new):
            pj = jnp.exp(sj - m_fin)
            l_fin = l_fin + pj
            acc = acc + pj * vn[j:j + 1, :]
        out = acc / l_fin
        o_ref[0] = jnp.concatenate(
            [jnp.sum(jnp.where(own, out[i * FOX_H:(i + 1) * FOX_H, :], 0.0), axis=0, keepdims=True)
             for i in range(seq)], axis=0)


def _fox_sample(page_table, qf, kn, vn, cc, cache_kt, cache_vt, e_pages, *, pp):
    b, seq, _ = qf.shape
    n_pages = page_table.shape[1]
    assert n_pages % pp == 0
    page = cache_kt.shape[2]
    n_chunks = n_pages // pp
    tok = lambda w: pl.BlockSpec((1, seq, w), lambda i, c, pt: (i, 0, 0))

    def paged(shape, k):
        return pl.BlockSpec((1,) + shape, lambda i, c, pt: (pt[i, (n_chunks - 1 - c) * pp + k], 0, 0))

    rows = seq * FOX_H
    grid_spec = pltpu.PrefetchScalarGridSpec(
        num_scalar_prefetch=1,
        grid=(b, n_chunks),
        in_specs=[tok(D_FOX), tok(D_FOX), tok(D_FOX), tok(FOX_H)]
        + [paged((D_FOX, page), k) for k in range(pp)]
        + [paged((D_FOX, page), k) for k in range(pp)]
        + [paged((FOX_H, 2 * page), k) for k in range(pp)],
        out_specs=tok(D_FOX),
        scratch_shapes=[pltpu.VMEM((rows, D_FOX), F32), pltpu.VMEM((rows, 1), F32), pltpu.VMEM((rows, 1), F32),
                        pltpu.VMEM((rows, D_FOX), F32), pltpu.VMEM((FOX_H, page), F32)],
    )
    return pl.pallas_call(
        functools.partial(_fox_sample_kernel, pp=pp, seq=seq, page=page),
        grid_spec=grid_spec,
        out_shape=jax.ShapeDtypeStruct((b, seq, D_FOX), F32),
        compiler_params=_params(("arbitrary", "arbitrary")),
        name="fox_sample",
    )(page_table, qf, kn, vn, cc, *([cache_kt] * pp), *([cache_vt] * pp), *([e_pages] * pp))


def _out_kernel(x_ref, yg_ref, of_ref, zf_ref, w_ref, gf_ref, y_ref):
    yf = (of_ref[...].astype(F32) * _silu(zf_ref[...].astype(F32))).astype(BF16)
    d = _dot(yg_ref[...], w_ref[0:D_GLA, :]) + _dot(yf, w_ref[D_GLA:D_GLA + D_FOX, :])
    xo = x_ref[...] + d
    ms = jnp.mean(xo * xo, axis=-1, keepdims=True)
    y_ref[...] = xo * lax.rsqrt(ms + RMS_EPS) * gf_ref[...]


def _out(x2d, yg, of, zf, w_out, g_final, *, tm):
    n, d = x2d.shape
    rows = lambda w: pl.BlockSpec((tm, w), lambda i: (i, 0))
    full = lambda a: pl.BlockSpec(a.shape, lambda i: (0, 0))
    return pl.pallas_call(
        _out_kernel,
        grid=(n // tm,),
        in_specs=[rows(d), rows(D_GLA), rows(D_FOX), rows(D_FOX), full(w_out), full(g_final)],
        out_specs=rows(d),
        out_shape=jax.ShapeDtypeStruct((n, d), F32),
        compiler_params=_params(("arbitrary",)),
        name="out",
    )(x2d, yg, of, zf, w_out, g_final)


def _regroup_weights(w_in, w_a2, b_a, b_f):
    o_alr = 2 * D_QK + 2 * D_GLA
    o_qf = o_alr + GLA_RANK
    o_fl = o_qf + 4 * D_FOX
    d = w_in.shape[0]
    w_main = jnp.concatenate([w_in[:, :o_alr], w_in[:, o_qf:o_fl]], axis=1).astype(BF16)
    w_kvt = w_in[:, o_qf:o_qf + 3 * D_FOX].T.astype(BF16)
    pad = LANE - FOX_H - GLA_RANK
    w_small = jnp.concatenate([w_in[:, o_fl:o_fl + FOX_H], w_in[:, o_alr:o_qf], jnp.zeros((d, pad), F32)],
                              axis=1).astype(BF16)
    w_a2p = jnp.concatenate([jnp.zeros((FOX_H, D_QK), F32), w_a2, jnp.zeros((pad, D_QK), F32)],
                            axis=0).astype(BF16)
    b_fp = jnp.concatenate([b_f, jnp.zeros((LANE - FOX_H,), F32)])[None, :]
    return w_main, w_kvt, w_small, w_a2p, b_a[None, :], b_fp


def kernel(x_prompt, x_sample, cache_k, cache_v, cache_logf, state_gla, page_table,
           g_norm, w_in, w_a2, b_a, b_f, gla_norm, w_out, g_final):
    depth = w_in.shape[0]
    assert depth == 1, "single-layer step"
    bp, tp, d = x_prompt.shape
    bs, ts, _ = x_sample.shape
    n_pool, page = cache_k.shape[1], cache_k.shape[2]
    l = 0
    weights = _regroup_weights(w_in[l], w_a2[l], b_a[l], b_f[l])
    g = g_norm[l][None, :]
    gn = gla_norm[l][None, :]
    w_o = w_out[l].astype(BF16)
    gf = g_final[None, :]

    xp = x_prompt.reshape(bp * tp, d)
    pr = _proj(xp, g, *weights, seq_len=tp, tm=min(512, tp))
    r3 = lambda a: a.reshape(bp, tp, a.shape[-1])
    yg, s_prompt = _gla_prompt(r3(pr["qg"]), r3(pr["kg"]), r3(pr["vg"]), r3(pr["zg"]), r3(pr["la"]), gn,
                               tb=min(256, tp))
    of = _fox_prompt(pr["qft"], r3(pr["kfb"]), pr["vfbt"], pr["ct"].reshape(bp, FOX_H // 2, 2, tp),
                     bq=min(256, tp), hp=2)
    y_prompt = _out(xp, yg.reshape(bp * tp, D_GLA), of.reshape(bp * tp, D_FOX), pr["zf"], w_o, gf,
                    tm=min(512, bp * tp))
    heads_last = lambda a: a.reshape(bp, FOX_H, FOX_HD, tp).transpose(0, 3, 1, 2)[None]
    k_prompt, v_prompt = heads_last(pr["kft"]), heads_last(pr["vft"])
    lf_prompt = pr["lft"].transpose(0, 2, 1)[None]

    xs = x_sample.reshape(bs * ts, d)
    sm = _proj(xs, g, *weights, seq_len=ts, tm=min(512, bs * ts))
    yg_s, s_sample = _gla_sample(sm["qg"], sm["kg"], sm["vg"], sm["zg"], sm["la"], gn, state_gla[l], seq=ts, nb=8)
    ckt = cache_k[l].transpose(0, 2, 3, 1).reshape(n_pool, D_FOX, page)
    cvt = cache_v[l].transpose(0, 2, 3, 1).reshape(n_pool, D_FOX, page)
    clf = cache_logf[l].transpose(0, 2, 1).reshape(n_pool * FOX_H, page)
    e_pages = _lf_prep(clf, tp=math.gcd(n_pool * FOX_H, 2048)).reshape(n_pool, FOX_H, 2 * page)
    s3 = lambda a: a.reshape(bs, ts, a.shape[-1])
    of_s = _fox_sample(page_table, s3(sm["qf"].astype(F32)), s3(sm["kf"]), s3(sm["vf"]), s3(sm["cc"]),
                       ckt, cvt, e_pages, pp=math.gcd(page_table.shape[1], 16))
    y_sample = _out(xs, yg_s, of_s.reshape(bs * ts, D_FOX), sm["zf"], w_o, gf, tm=min(512, bs * ts))

    return (y_prompt.reshape(bp, tp, d), y_sample.reshape(bs, ts, d),
            k_prompt, v_prompt, lf_prompt, s_prompt[None],
            sm["kf"].reshape(1, bs, ts, FOX_H, FOX_HD), sm["vf"].reshape(1, bs, ts, FOX_H, FOX_HD),
            sm["lf"].reshape(1, bs, ts, FOX_H), s_sample[None])
```

```python
import functools
import math

import jax
import jax.numpy as jnp
from jax import lax
from jax.experimental import pallas as pl
from jax.experimental.pallas import tpu as pltpu

F32 = jnp.float32
BF16 = jnp.bfloat16

GLA_H = 4
GLA_DK = 64
GLA_DV = 128
GLA_RANK = 16
GLA_GATE_NORM = 16.0
GLA_CHUNK = 32
FOX_H = 8
FOX_HD = 64
RMS_EPS = 1e-6
_LOG2E = 1.4426950408889634

D_QK = GLA_H * GLA_DK
D_GLA = GLA_H * GLA_DV
D_FOX = FOX_H * FOX_HD
LANE = 128
SUBLANE = 8
VMEM_LIMIT = 56 * 1024 * 1024

_OFF_QG, _OFF_KG, _OFF_VG, _OFF_ZG = 0, 256, 512, 1024
_OFF_QF, _OFF_KF, _OFF_VF, _OFF_ZF, _OFF_END = 1536, 2048, 2560, 3072, 3584


def _dot(a, b):
    return jnp.dot(a, b, preferred_element_type=F32)


def _dot_nt(a, b):
    return lax.dot_general(a, b, (((1,), (1,)), ((), ())), preferred_element_type=F32)


def _dot_tn(a, b):
    return lax.dot_general(a, b, (((0,), (0,)), ((), ())), preferred_element_type=F32)


def _split3(x):
    hi = x.astype(BF16)
    r = x - hi.astype(F32)
    mid = r.astype(BF16)
    lo = (r - mid.astype(F32)).astype(BF16)
    return hi, mid, lo


def _dot_exact_mask(mask01, x):
    hi, mid, lo = _split3(x)
    return _dot(mask01, hi) + _dot(mask01, mid) + _dot(mask01, lo)


def _log_sigmoid(x):
    return jnp.minimum(x, 0.0) - jnp.log1p(jnp.exp(-jnp.abs(x)))


def _silu(x):
    return x / (1.0 + jnp.exp(-x))


def _log2(n):
    k = n.bit_length() - 1
    assert (1 << k) == n, f"{n} must be a power of two"
    return k


def _params(sem):
    return pltpu.CompilerParams(dimension_semantics=sem, vmem_limit_bytes=VMEM_LIMIT)


def _proj_kernel(x_ref, g_ref, wm_ref, wkvt_ref, ws_ref, wa_ref, ba_ref, bf_ref, *refs, tm, seq_len, names):
    out = dict(zip(names, refs))
    carry_ref = refs[len(names)]
    x = x_ref[...]
    ms = jnp.mean(x * x, axis=-1, keepdims=True)
    h = (x * lax.rsqrt(ms + RMS_EPS) * g_ref[...]).astype(BF16)

    def grp(a, b):
        return _dot(h, wm_ref[:, a:b])

    out["qg"][...] = (grp(_OFF_QG, _OFF_KG) * (GLA_DK ** -0.5)).astype(BF16)
    out["kg"][...] = grp(_OFF_KG, _OFF_VG).astype(BF16)
    out["vg"][...] = grp(_OFF_VG, _OFF_ZG).astype(BF16)
    out["zg"][...] = grp(_OFF_ZG, _OFF_QF).astype(BF16)
    out["zf"][...] = grp(_OFF_ZF, _OFF_END).astype(BF16)

    small = _dot(h, ws_ref[...])
    la = _log_sigmoid(_dot(small.astype(BF16), wa_ref[...]) + ba_ref[...])
    out["la"][...] = la * (1.0 / GLA_GATE_NORM)
    lf = _log_sigmoid(small + bf_ref[...])

    row = lax.broadcasted_iota(jnp.int32, (tm, tm), 0)
    col = lax.broadcasted_iota(jnp.int32, (tm, tm), 1)
    keep = col <= row
    if seq_len < tm:
        sh = _log2(seq_len)
        keep = jnp.logical_and(keep, lax.shift_right_logical(row, sh) == lax.shift_right_logical(col, sh))
    tri = jnp.where(keep, 1.0, 0.0).astype(BF16)
    c = _dot_exact_mask(tri, lf)

    if seq_len >= tm:
        steps = seq_len // tm
        if steps > 1:
            @pl.when(pl.program_id(0) % steps == 0)
            def _():
                carry_ref[...] = jnp.zeros_like(carry_ref)

            c = c + carry_ref[0:1, :]
            carry_ref[0:1, :] = c[tm - 1:tm, :]
        qkvt = _dot_nt(wkvt_ref[...], h)
        out["qft"][0] = (qkvt[0:D_FOX] * (FOX_HD ** -0.5 * _LOG2E)).astype(BF16)
        out["kft"][0] = qkvt[D_FOX:2 * D_FOX]
        out["vft"][0] = qkvt[2 * D_FOX:3 * D_FOX]
        out["vfbt"][0] = qkvt[2 * D_FOX:3 * D_FOX].astype(BF16)
        out["kfb"][...] = grp(_OFF_KF, _OFF_VF).astype(BF16)
        out["lft"][0] = lf.T[0:FOX_H, :]
        out["ct"][0] = c.T[0:FOX_H, :]
    else:
        out["qf"][...] = (grp(_OFF_QF, _OFF_KF) * (FOX_HD ** -0.5)).astype(BF16)
        out["kf"][...] = grp(_OFF_KF, _OFF_VF)
        out["vf"][...] = grp(_OFF_VF, _OFF_ZF)
        out["lf"][...] = lf[:, 0:FOX_H]
        out["cc"][...] = c[:, 0:FOX_H]


def _proj(x2d, g, w_main, w_kvt, w_small, w_a2p, b_a, b_fp, *, seq_len, tm):
    n, d = x2d.shape
    assert n % tm == 0 and (seq_len % tm == 0 or tm % seq_len == 0)
    rows = lambda w: pl.BlockSpec((tm, w), lambda i: (i, 0))
    full = lambda a: pl.BlockSpec(a.shape, lambda i: (0, 0))
    tok = lambda w, dt: (jax.ShapeDtypeStruct((n, w), dt), rows(w))
    outs = {"qg": tok(D_QK, BF16), "kg": tok(D_QK, BF16), "vg": tok(D_GLA, BF16), "zg": tok(D_GLA, BF16),
            "la": tok(D_QK, F32), "zf": tok(D_FOX, BF16)}
    if seq_len >= tm:
        steps = seq_len // tm
        nseq = n // seq_len
        feat = lambda w, dt: (jax.ShapeDtypeStruct((nseq, w, seq_len), dt),
                              pl.BlockSpec((1, w, tm), lambda i: (i // steps, 0, i % steps)))
        outs.update({"qft": feat(D_FOX, BF16), "kfb": tok(D_FOX, BF16), "kft": feat(D_FOX, F32),
                     "vft": feat(D_FOX, F32), "vfbt": feat(D_FOX, BF16),
                     "lft": feat(FOX_H, F32), "ct": feat(FOX_H, F32)})
    else:
        outs.update({"qf": tok(D_FOX, BF16), "kf": tok(D_FOX, F32), "vf": tok(D_FOX, F32),
                     "lf": tok(FOX_H, F32), "cc": tok(FOX_H, F32)})
    names = tuple(outs)
    res = pl.pallas_call(
        functools.partial(_proj_kernel, tm=tm, seq_len=seq_len, names=names),
        grid=(n // tm,),
        in_specs=[rows(d), full(g), full(w_main), full(w_kvt), full(w_small), full(w_a2p), full(b_a), full(b_fp)],
        out_specs=[outs[k][1] for k in names],
        out_shape=[outs[k][0] for k in names],
        scratch_shapes=[pltpu.VMEM((SUBLANE, LANE), F32)],
        compiler_params=_params(("arbitrary",)),
        name="proj",
    )(x2d, g, w_main, w_kvt, w_small, w_a2p, b_a, b_fp)
    return dict(zip(names, res))


def _gla_prepare(q_ref, k_ref, la_ref, rows, chunk):
    la = la_ref[...].reshape(rows, D_QK)
    sh = _log2(chunk)
    row = lax.broadcasted_iota(jnp.int32, (rows, rows), 0)
    col = lax.broadcasted_iota(jnp.int32, (rows, rows), 1)
    same = lax.shift_right_logical(row, sh) == lax.shift_right_logical(col, sh)
    lower = jnp.logical_and(same, col <= row)
    upper = jnp.logical_and(same, col > row)
    cum = _dot_exact_mask(jnp.where(lower, 1.0, 0.0).astype(BF16), la)
    suf = _dot_exact_mask(jnp.where(upper, 1.0, 0.0).astype(BF16), la)
    q = q_ref[...].reshape(rows, D_QK).astype(F32)
    k = k_ref[...].reshape(rows, D_QK).astype(F32)
    qt = (q * jnp.exp(cum)).astype(BF16)
    kt = (k * jnp.exp(-cum)).astype(BF16)
    kp = (k * jnp.exp(suf)).astype(BF16)
    dec = jnp.exp(cum + suf)
    return qt, kt, kp, dec, lower


def _gla_intra(qtp, ktp, vp, lower):
    lane = lax.broadcasted_iota(jnp.int32, (1, 2 * GLA_DK), 1)
    first = lane < GLA_DK
    zero = jnp.zeros_like(qtp)
    a0 = jnp.where(lower, _dot_nt(jnp.where(first, qtp, zero), ktp), 0.0).astype(BF16)
    a1 = jnp.where(lower, _dot_nt(jnp.where(first, zero, qtp), ktp), 0.0).astype(BF16)
    vlane = lax.broadcasted_iota(jnp.int32, (1, 2 * GLA_DV), 1)
    vzero = jnp.zeros_like(vp)
    v0 = jnp.where(vlane < GLA_DV, vp, vzero)
    v1 = jnp.where(vlane < GLA_DV, vzero, vp)
    return _dot(a0, v0) + _dot(a1, v1)


def _pair_blockdiag_mask():
    r = lax.broadcasted_iota(jnp.int32, (2 * GLA_DV, 2 * GLA_DK), 0)
    c = lax.broadcasted_iota(jnp.int32, (2 * GLA_DV, 2 * GLA_DK), 1)
    return lax.shift_right_logical(r, _log2(GLA_DV)) == lax.shift_right_logical(c, _log2(GLA_DK))


def _gla_finish(o, z_ref, gn_ref, y_ref, rows):
    ys = []
    for h in range(GLA_H):
        oh = o[:, h * GLA_DV:(h + 1) * GLA_DV]
        ms = jnp.mean(oh * oh, axis=-1, keepdims=True)
        ys.append(oh * lax.rsqrt(ms + RMS_EPS) * gn_ref[...])
    y = jnp.concatenate(ys, axis=1) * _silu(z_ref[...].reshape(rows, D_GLA).astype(F32))
    y_ref[...] = y.astype(BF16).reshape(y_ref.shape)


def _state_to_pairs(s_ref, idx, p):
    sa = s_ref[idx + (2 * p,)]
    sb = s_ref[idx + (2 * p + 1,)]
    z = jnp.zeros_like(sa)
    sbd = jnp.concatenate([jnp.concatenate([sa, z], axis=1), jnp.concatenate([z, sb], axis=1)], axis=0)
    return sbd.T


def _pairs_to_state(st, s_ref, idx, p):
    sbd = st.T
    s_ref[idx + (2 * p,)] = sbd[0:GLA_DK, 0:GLA_DV]
    s_ref[idx + (2 * p + 1,)] = sbd[GLA_DK:2 * GLA_DK, GLA_DV:2 * GLA_DV]


def _gla_prompt_kernel(q_ref, k_ref, v_ref, z_ref, la_ref, gn_ref, y_ref, sout_ref, st_ref, *, tb, chunk):
    t = pl.program_id(1)

    @pl.when(t == 0)
    def _():
        st_ref[...] = jnp.zeros_like(st_ref)

    qt, kt, kp, dec, lower = _gla_prepare(q_ref, k_ref, la_ref, tb, chunk)
    v = v_ref[0]
    bd = _pair_blockdiag_mask()
    o_parts = []
    for p in range(GLA_H // 2):
        ql = slice(p * 2 * GLA_DK, (p + 1) * 2 * GLA_DK)
        qtp, ktp, kpp = qt[:, ql], kt[:, ql], kp[:, ql]
        vp = v[:, p * 2 * GLA_DV:(p + 1) * 2 * GLA_DV]
        o_intra = _gla_intra(qtp, ktp, vp, lower)
        s = st_ref[p]
        inter = []
        for c in range(tb // chunk):
            sl = slice(c * chunk, (c + 1) * chunk)
            inter.append(_dot_nt(qtp[sl], s.astype(BF16)))
            upd = _dot_tn(vp[sl], kpp[sl])
            s = s * dec[c * chunk:c * chunk + 1, ql] + jnp.where(bd, upd, 0.0)
        st_ref[p] = s
        o_parts.append(o_intra + jnp.concatenate(inter, axis=0))
    _gla_finish(jnp.concatenate(o_parts, axis=1), z_ref, gn_ref, y_ref, tb)

    @pl.when(t == pl.num_programs(1) - 1)
    def _():
        for p in range(GLA_H // 2):
            _pairs_to_state(st_ref[p], sout_ref, (0,), p)


def _gla_prompt(qg, kg, vg, zg, la, gn, *, tb):
    b, t, _ = qg.shape
    assert t % tb == 0 and tb % GLA_CHUNK == 0
    tok = lambda w: pl.BlockSpec((1, tb, w), lambda i, j: (i, j, 0))
    return pl.pallas_call(
        functools.partial(_gla_prompt_kernel, tb=tb, chunk=GLA_CHUNK),
        grid=(b, t // tb),
        in_specs=[tok(D_QK), tok(D_QK), tok(D_GLA), tok(D_GLA), tok(D_QK),
                  pl.BlockSpec((1, GLA_DV), lambda i, j: (0, 0))],
        out_specs=[tok(D_GLA), pl.BlockSpec((1, GLA_H, GLA_DK, GLA_DV), lambda i, j: (i, 0, 0, 0))],
        out_shape=[jax.ShapeDtypeStruct((b, t, D_GLA), BF16),
                   jax.ShapeDtypeStruct((b, GLA_H, GLA_DK, GLA_DV), F32)],
        scratch_shapes=[pltpu.VMEM((GLA_H // 2, 2 * GLA_DV, 2 * GLA_DK), F32)],
        compiler_params=_params(("arbitrary", "arbitrary")),
        name="gla_prompt",
    )(qg, kg, vg, zg, la, gn)


def _gla_sample_kernel(q_ref, k_ref, v_ref, z_ref, la_ref, gn_ref, s0_ref, y_ref, sout_ref, *, nb, seq):
    rows = nb * seq
    qt, kt, kp, dec, lower = _gla_prepare(q_ref, k_ref, la_ref, rows, seq)
    v = v_ref[...]
    bd = _pair_blockdiag_mask()
    seq_of_row = lax.shift_right_logical(lax.broadcasted_iota(jnp.int32, (rows, 1), 0), _log2(seq))
    o_parts = []
    for p in range(GLA_H // 2):
        ql = slice(p * 2 * GLA_DK, (p + 1) * 2 * GLA_DK)
        qtp, ktp, kpp = qt[:, ql], kt[:, ql], kp[:, ql]
        vp = v[:, p * 2 * GLA_DV:(p + 1) * 2 * GLA_DV]
        o = _gla_intra(qtp, ktp, vp, lower)
        for i in range(nb):
            mine = seq_of_row == i
            st = _state_to_pairs(s0_ref, (i,), p)
            o = o + jnp.where(mine, _dot_nt(qtp, st.astype(BF16)), 0.0)
            upd = _dot_tn(jnp.where(mine, vp, jnp.zeros_like(vp)), kpp)
            st = st * dec[i * seq:i * seq + 1, ql] + jnp.where(bd, upd, 0.0)
            _pairs_to_state(st, sout_ref, (i,), p)
        o_parts.append(o)
    _gla_finish(jnp.concatenate(o_parts, axis=1), z_ref, gn_ref, y_ref, rows)


def _gla_sample(qg, kg, vg, zg, la, gn, s0, *, seq, nb):
    n = qg.shape[0]
    b = n // seq
    assert b % nb == 0 and (nb * seq) % SUBLANE == 0
    rows = nb * seq
    tok = lambda w: pl.BlockSpec((rows, w), lambda i: (i, 0))
    st = pl.BlockSpec((nb, GLA_H, GLA_DK, GLA_DV), lambda i: (i, 0, 0, 0))
    return pl.pallas_call(
        functools.partial(_gla_sample_kernel, nb=nb, seq=seq),
        grid=(b // nb,),
        in_specs=[tok(D_QK), tok(D_QK), tok(D_GLA), tok(D_GLA), tok(D_QK),
                  pl.BlockSpec((1, GLA_DV), lambda i: (0, 0)), st],
        out_specs=[tok(D_GLA), st],
        out_shape=[jax.ShapeDtypeStruct((n, D_GLA), BF16),
                   jax.ShapeDtypeStruct((b, GLA_H, GLA_DK, GLA_DV), F32)],
        compiler_params=_params(("arbitrary",)),
        name="gla_sample",
    )(qg, kg, vg, zg, la, gn, s0)


def _fox_prompt_kernel(q_ref, k_ref, v_ref, ct_ref, o_ref, kaug_ref, qs_ref, s_ref, m_ref, l_ref, acc_ref,
                       *, seq_len, bq, hp):
    pw = 2 * FOX_HD
    n_split = 3
    tc = min(512, seq_len)
    sub = lax.broadcasted_iota(jnp.int32, (pw, 1), 0)
    first = sub < FOX_HD
    bk = bq // 2
    nq = seq_len // bq
    key = lax.broadcasted_iota(jnp.int32, (bk, bq), 0)
    qry = lax.broadcasted_iota(jnp.int32, (bk, bq), 1)
    diag_masks = (key <= qry, key + bk <= qry)
    sel = [jnp.broadcast_to(jnp.where(jnp.logical_and(sub >= n_split * hl, sub < n_split * (hl + 1)),
                                      1.0, 0.0).astype(BF16), (pw, bq)) for hl in range(2)]

    for pi in range(hp):
        kaug_ref[pi, :, 0:pw] = k_ref[0, :, pi * pw:(pi + 1) * pw]
        parts = []
        for hl in range(2):
            parts.extend(p.astype(F32) for p in _split3(ct_ref[0, pi, hl:hl + 1, :] * (-_LOG2E)))
        parts.append(jnp.zeros((pw - 2 * n_split, seq_len), F32))
        bias_rows = jnp.concatenate(parts, axis=0)
        for t0 in range(0, seq_len, tc):
            kaug_ref[pi, t0:t0 + tc, pw:2 * pw] = bias_rows[:, t0:t0 + tc].T.astype(BF16)

    def load_queries(i, buf):
        q0 = pl.multiple_of(i * bq, bq)
        for pi in range(hp):
            qt = q_ref[0, pi * pw:(pi + 1) * pw, pl.ds(q0, bq)]
            zero = jnp.zeros_like(qt)
            qs_ref[buf, 2 * pi] = jnp.concatenate([jnp.where(first, qt, zero), sel[0]], axis=0)
            qs_ref[buf, 2 * pi + 1] = jnp.concatenate([jnp.where(first, zero, qt), sel[1]], axis=0)

    def scores(j, slot, buf):
        ks = pl.multiple_of(j * bk, bk)
        for h in range(2 * hp):
            s_ref[slot, h] = _dot(kaug_ref[h // 2, pl.ds(ks, bk), :], qs_ref[buf, h])

    def consume(j, slot, mask=None):
        ks = pl.multiple_of(j * bk, bk)
        for h in range(2 * hp):
            s = s_ref[slot, h]
            if mask is not None:
                s = jnp.where(mask, s, -jnp.inf)
            m = m_ref[h]
            mn = jnp.maximum(m, jnp.max(s, axis=0, keepdims=True))
            alpha = jnp.exp2(m - mn)
            pr = jnp.exp2(s - mn)
            m_ref[h] = mn
            l_ref[h] = l_ref[h] * alpha + jnp.sum(pr, axis=0, keepdims=True)
            vb = v_ref[0, h * FOX_HD:(h + 1) * FOX_HD, pl.ds(ks, bk)]
            acc_ref[h] = acc_ref[h] * alpha + _dot(vb, pr.astype(BF16))

    def q_block(i, buf):
        m_ref[...] = jnp.full_like(m_ref, -jnp.inf)
        l_ref[...] = jnp.zeros_like(l_ref)
        acc_ref[...] = jnp.zeros_like(acc_ref)

        def kv_pair(jj, _):
            j = 2 * jj
            scores(j + 1, 1, buf)
            consume(j, 0)
            scores(j + 2, 0, buf)
            consume(j + 1, 1)
            return 0

        lax.fori_loop(0, i, kv_pair, 0)
        scores(2 * i + 1, 1, buf)
        consume(2 * i, 0, diag_masks[0])
        load_queries(jnp.minimum(i + 1, nq - 1), 1 - buf)
        scores(0, 0, 1 - buf)
        consume(2 * i + 1, 1, diag_masks[1])
        q0 = pl.multiple_of(i * bq, bq)
        for pi in range(hp):
            pair_t = jnp.concatenate([acc_ref[2 * pi] / l_ref[2 * pi],
                                      acc_ref[2 * pi + 1] / l_ref[2 * pi + 1]], axis=0)
            o_ref[0, pl.ds(q0, bq), pi * pw:(pi + 1) * pw] = pair_t.T.astype(BF16)

    def q_pair(u, _):
        q_block(2 * u, 0)
        q_block(2 * u + 1, 1)
        return 0

    load_queries(0, 0)
    scores(0, 0, 0)
    lax.fori_loop(0, nq // 2, q_pair, 0)


def _fox_prompt(qft, kf, vft, ct, *, bq, hp):
    b, _, t = qft.shape
    pairs = FOX_H // 2
    assert t % (2 * bq) == 0 and pairs % hp == 0
    pw = 2 * FOX_HD
    tok = pl.BlockSpec((1, t, hp * pw), lambda i, p: (i, 0, p))
    feat = pl.BlockSpec((1, hp * pw, t), lambda i, p: (i, p, 0))
    return pl.pallas_call(
        functools.partial(_fox_prompt_kernel, seq_len=t, bq=bq, hp=hp),
        grid=(b, pairs // hp),
        in_specs=[feat, tok, feat, pl.BlockSpec((1, hp, 2, t), lambda i, p: (i, p, 0, 0))],
        out_specs=tok,
        out_shape=jax.ShapeDtypeStruct((b, t, D_FOX), BF16),
        scratch_shapes=[pltpu.VMEM((hp, t, 2 * pw), BF16),
                        pltpu.VMEM((2, 2 * hp, 2 * pw, bq), BF16),
                        pltpu.VMEM((2, 2 * hp, bq // 2, bq), F32),
                        pltpu.VMEM((2 * hp, 1, bq), F32),
                        pltpu.VMEM((2 * hp, 1, bq), F32),
                        pltpu.VMEM((2 * hp, FOX_HD, bq), F32)],
        compiler_params=_params(("arbitrary", "arbitrary")),
        name="fox_prompt",
    )(qft, kf, vft, ct)


def _lf_prep_kernel(lf_ref, m_ref, o_ref):
    hi, mid, lo = _split3(lf_ref[...])
    o_ref[...] = _dot(hi, m_ref[...]) + _dot(mid, m_ref[...]) + _dot(lo, m_ref[...])


def _lf_prep(lf_rows, *, tp):
    n, page = lf_rows.shape
    src = lax.broadcasted_iota(jnp.int32, (page, 2 * page), 0)
    dst = lax.broadcasted_iota(jnp.int32, (page, 2 * page), 1)
    mask = jnp.where(jnp.logical_or(dst >= page, src > dst), 1.0, 0.0).astype(BF16)
    assert n % tp == 0
    return pl.pallas_call(
        _lf_prep_kernel,
        grid=(n // tp,),
        in_specs=[pl.BlockSpec((tp, page), lambda i: (i, 0)), pl.BlockSpec(mask.shape, lambda i: (0, 0))],
        out_specs=pl.BlockSpec((tp, 2 * page), lambda i: (i, 0)),
        out_shape=jax.ShapeDtypeStruct((n, 2 * page), F32),
        compiler_params=_params(("arbitrary",)),
        name="lf_prep",
    )(lf_rows, mask)


def _fox_sample_kernel(pt_ref, q_ref, kn_ref, vn_ref, cc_ref, *rest, pp, seq, page):
    k_refs, v_refs, e_refs = rest[:pp], rest[pp:2 * pp], rest[2 * pp:3 * pp]
    o_ref, qbd_ref, m_ref, l_ref, acc_ref, car_ref = rest[3 * pp:]
    del pt_ref
    c = pl.program_id(1)
    rows = seq * FOX_H
    sub = lax.broadcasted_iota(jnp.int32, (FOX_H, D_FOX), 0)
    lane = lax.broadcasted_iota(jnp.int32, (FOX_H, D_FOX), 1)
    own = lax.shift_right_logical(lane, _log2(FOX_HD)) == sub

    @pl.when(c == 0)
    def _():
        q = q_ref[0]
        qbd_ref[...] = jnp.concatenate([jnp.where(own, q[i:i + 1, :], 0.0) for i in range(seq)], axis=0)
        m_ref[...] = jnp.full_like(m_ref, -jnp.inf)
        l_ref[...] = jnp.zeros_like(l_ref)
        acc_ref[...] = jnp.zeros_like(acc_ref)
        car_ref[...] = jnp.zeros_like(car_ref)

    qbd = qbd_ref[...]
    carry = car_ref[...]
    bias = [None] * pp
    for i in reversed(range(pp)):
        e = e_refs[i][0]
        bias[i] = jnp.concatenate([e[:, 0:page] + carry] * seq, axis=0)
        carry = carry + e[:, page:2 * page]
    car_ref[...] = carry
    kcat = jnp.concatenate([k_refs[i][0] for i in range(pp)], axis=1)
    s = _dot(qbd, kcat) + jnp.concatenate(bias, axis=1)
    m_old = m_ref[...]
    mn = jnp.maximum(m_old, jnp.max(s, axis=-1, keepdims=True))
    alpha = jnp.exp(m_old - mn)
    pr = jnp.exp(s - mn)
    l_ref[...] = l_ref[...] * alpha + jnp.sum(pr, axis=-1, keepdims=True)
    vcat = jnp.concatenate([v_refs[i][0] for i in range(pp)], axis=1)
    acc_ref[...] = acc_ref[...] * alpha + _dot_nt(pr, vcat)
    m_ref[...] = mn

    @pl.when(c == pl.num_programs(1) - 1)
    def _():
        kn = kn_ref[0]
        vn = vn_ref[0]
        cc = cc_ref[0]
        eye = (lax.broadcasted_iota(jnp.int32, (FOX_H, FOX_H), 0)
               == lax.broadcasted_iota(jnp.int32, (FOX_H, FOX_H), 1))
        q_of_row = lax.shift_right_logical(lax.broadcasted_iota(jnp.int32, (rows, 1), 0), _log2(FOX_H))
        s_new = []
        for j in range(seq):
            cj = jnp.sum(jnp.where(eye, cc[j:j + 1, :], 0.0), axis=-1, keepdims=True)
            sj = jnp.sum(qbd * kn[j:j + 1, :], axis=-1, keepdims=True) - jnp.concatenate([cj] * seq, axis=0)
            s_new.append(jnp.where(q_of_row >= j, sj, -jnp.inf))
        m_prev = m_ref[...]
        m_fin = m_prev
        for sj in s_new:
            m_fin = jnp.maximum(m_fin, sj)
        a_fin = jnp.exp(m_prev - m_fin)
        l_fin = l_ref[...] * a_fin
        acc = acc_ref[...] * a_fin
        for j, sj in enumerate(s_new):
            pj = jnp.exp(sj - m_fin)
            l_fin = l_fin + pj
            acc = acc + pj * vn[j:j + 1, :]
        out = acc / l_fin
        o_ref[0] = jnp.concatenate(
            [jnp.sum(jnp.where(own, out[i * FOX_H:(i + 1) * FOX_H, :], 0.0), axis=0, keepdims=True)
             for i in range(seq)], axis=0)


def _fox_sample(page_table, qf, kn, vn, cc, cache_kt, cache_vt, e_pages, *, pp):
    b, seq, _ = qf.shape
    n_pages = page_table.shape[1]
    assert n_pages % pp == 0
    page = cache_kt.shape[2]
    n_chunks = n_pages // pp
    tok = lambda w: pl.BlockSpec((1, seq, w), lambda i, c, pt: (i, 0, 0))

    def paged(shape, k):
        return pl.BlockSpec((1,) + shape, lambda i, c, pt: (pt[i, (n_chunks - 1 - c) * pp + k], 0, 0))

    rows = seq * FOX_H
    grid_spec = pltpu.PrefetchScalarGridSpec(
        num_scalar_prefetch=1,
        grid=(b, n_chunks),
        in_specs=[tok(D_FOX), tok(D_FOX), tok(D_FOX), tok(FOX_H)]
        + [paged((D_FOX, page), k) for k in range(pp)]
        + [paged((D_FOX, page), k) for k in range(pp)]
        + [paged((FOX_H, 2 * page), k) for k in range(pp)],
        out_specs=tok(D_FOX),
        scratch_shapes=[pltpu.VMEM((rows, D_FOX), F32), pltpu.VMEM((rows, 1), F32), pltpu.VMEM((rows, 1), F32),
                        pltpu.VMEM((rows, D_FOX), F32), pltpu.VMEM((FOX_H, page), F32)],
    )
    return pl.pallas_call(
        functools.partial(_fox_sample_kernel, pp=pp, seq=seq, page=page),
        grid_spec=grid_spec,
        out_shape=jax.ShapeDtypeStruct((b, seq, D_FOX), F32),
        compiler_params=_params(("arbitrary", "arbitrary")),
        name="fox_sample",
    )(page_table, qf, kn, vn, cc, *([cache_kt] * pp), *([cache_vt] * pp), *([e_pages] * pp))


def _out_kernel(x_ref, yg_ref, of_ref, zf_ref, w_ref, gf_ref, y_ref):
    yf = (of_ref[...].astype(F32) * _silu(zf_ref[...].astype(F32))).astype(BF16)
    d = _dot(yg_ref[...], w_ref[0:D_GLA, :]) + _dot(yf, w_ref[D_GLA:D_GLA + D_FOX, :])
    xo = x_ref[...] + d
    ms = jnp.mean(xo * xo, axis=-1, keepdims=True)
    y_ref[...] = xo * lax.rsqrt(ms + RMS_EPS) * gf_ref[...]


def _out(x2d, yg, of, zf, w_out, g_final, *, tm):
    n, d = x2d.shape
    rows = lambda w: pl.BlockSpec((tm, w), lambda i: (i, 0))
    full = lambda a: pl.BlockSpec(a.shape, lambda i: (0, 0))
    return pl.pallas_call(
        _out_kernel,
        grid=(n // tm,),
        in_specs=[rows(d), rows(D_GLA), rows(D_FOX), rows(D_FOX), full(w_out), full(g_final)],
        out_specs=rows(d),
        out_shape=jax.ShapeDtypeStruct((n, d), F32),
        compiler_params=_params(("arbitrary",)),
        name="out",
    )(x2d, yg, of, zf, w_out, g_final)


def _regroup_weights(w_in, w_a2, b_a, b_f):
    o_alr = 2 * D_QK + 2 * D_GLA
    o_qf = o_alr + GLA_RANK
    o_fl = o_qf + 4 * D_FOX
    d = w_in.shape[0]
    w_main = jnp.concatenate([w_in[:, :o_alr], w_in[:, o_qf:o_fl]], axis=1).astype(BF16)
    w_kvt = w_in[:, o_qf:o_qf + 3 * D_FOX].T.astype(BF16)
    pad = LANE - FOX_H - GLA_RANK
    w_small = jnp.concatenate([w_in[:, o_fl:o_fl + FOX_H], w_in[:, o_alr:o_qf], jnp.zeros((d, pad), F32)],
                              axis=1).astype(BF16)
    w_a2p = jnp.concatenate([jnp.zeros((FOX_H, D_QK), F32), w_a2, jnp.zeros((pad, D_QK), F32)],
                            axis=0).astype(BF16)
    b_fp = jnp.concatenate([b_f, jnp.zeros((LANE - FOX_H,), F32)])[None, :]
    return w_main, w_kvt, w_small, w_a2p, b_a[None, :], b_fp


def kernel(x_prompt, x_sample, cache_k, cache_v, cache_logf, state_gla, page_table,
           g_norm, w_in, w_a2, b_a, b_f, gla_norm, w_out, g_final):
    depth = w_in.shape[0]
    assert depth == 1, "single-layer step"
    bp, tp, d = x_prompt.shape
    bs, ts, _ = x_sample.shape
    n_pool, page = cache_k.shape[1], cache_k.shape[2]
    l = 0
    weights = _regroup_weights(w_in[l], w_a2[l], b_a[l], b_f[l])
    g = g_norm[l][None, :]
    gn = gla_norm[l][None, :]
    w_o = w_out[l].astype(BF16)
    gf = g_final[None, :]

    xp = x_prompt.reshape(bp * tp, d)
    pr = _proj(xp, g, *weights, seq_len=tp, tm=min(512, tp))
    r3 = lambda a: a.reshape(bp, tp, a.shape[-1])
    yg, s_prompt = _gla_prompt(r3(pr["qg"]), r3(pr["kg"]), r3(pr["vg"]), r3(pr["zg"]), r3(pr["la"]), gn,
                               tb=min(256, tp))
    of = _fox_prompt(pr["qft"], r3(pr["kfb"]), pr["vfbt"], pr["ct"].reshape(bp, FOX_H // 2, 2, tp),
                     bq=min(512, tp // 2), hp=2)
    y_prompt = _out(xp, yg.reshape(bp * tp, D_GLA), of.reshape(bp * tp, D_FOX), pr["zf"], w_o, gf,
                    tm=min(512, bp * tp))
    heads_last = lambda a: a.reshape(bp, FOX_H, FOX_HD, tp).transpose(0, 3, 1, 2)[None]
    k_prompt, v_prompt = heads_last(pr["kft"]), heads_last(pr["vft"])
    lf_prompt = pr["lft"].transpose(0, 2, 1)[None]

    xs = x_sample.reshape(bs * ts, d)
    sm = _proj(xs, g, *weights, seq_len=ts, tm=min(512, bs * ts))
    yg_s, s_sample = _gla_sample(sm["qg"], sm["kg"], sm["vg"], sm["zg"], sm["la"], gn, state_gla[l], seq=ts, nb=8)
    ckt = cache_k[l].transpose(0, 2, 3, 1).reshape(n_pool, D_FOX, page)
    cvt = cache_v[l].transpose(0, 2, 3, 1).reshape(n_pool, D_FOX, page)
    clf = cache_logf[l].transpose(0, 2, 1).reshape(n_pool * FOX_H, page)
    e_pages = _lf_prep(clf, tp=math.gcd(n_pool * FOX_H, 2048)).reshape(n_pool, FOX_H, 2 * page)
    s3 = lambda a: a.reshape(bs, ts, a.shape[-1])
    of_s = _fox_sample(page_table, s3(sm["qf"].astype(F32)), s3(sm["kf"]), s3(sm["vf"]), s3(sm["cc"]),
                       ckt, cvt, e_pages, pp=math.gcd(page_table.shape[1], 16))
    y_sample = _out(xs, yg_s, of_s.reshape(bs * ts, D_FOX), sm["zf"], w_o, gf, tm=min(512, bs * ts))

    return (y_prompt.reshape(bp, tp, d), y_sample.reshape(bs, ts, d),
            k_prompt, v_prompt, lf_prompt, s_prompt[None],
            sm["kf"].reshape(1, bs, ts, FOX_H, FOX_HD), sm["vf"].reshape(1, bs, ts, FOX_H, FOX_HD),
            sm["lf"].reshape(1, bs, ts, FOX_H), s_sample[None])
```

```python
import functools
import math

import jax
import jax.numpy as jnp
from jax import lax
from jax.experimental import pallas as pl
from jax.experimental.pallas import tpu as pltpu

F32 = jnp.float32
BF16 = jnp.bfloat16

GLA_H = 4
GLA_DK = 64
GLA_DV = 128
GLA_RANK = 16
GLA_GATE_NORM = 16.0
GLA_CHUNK = 32
FOX_H = 8
FOX_HD = 64
RMS_EPS = 1e-6
_LOG2E = 1.4426950408889634

D_QK = GLA_H * GLA_DK
D_GLA = GLA_H * GLA_DV
D_FOX = FOX_H * FOX_HD
LANE = 128
SUBLANE = 8
VMEM_LIMIT = 56 * 1024 * 1024

_OFF_QG, _OFF_KG, _OFF_VG, _OFF_ZG = 0, 256, 512, 1024
_OFF_QF, _OFF_KF, _OFF_VF, _OFF_ZF, _OFF_END = 1536, 2048, 2560, 3072, 3584


def _dot(a, b):
    return jnp.dot(a, b, preferred_element_type=F32)


def _dot_nt(a, b):
    return lax.dot_general(a, b, (((1,), (1,)), ((), ())), preferred_element_type=F32)


def _dot_tn(a, b):
    return lax.dot_general(a, b, (((0,), (0,)), ((), ())), preferred_element_type=F32)


def _split3(x):
    hi = x.astype(BF16)
    r = x - hi.astype(F32)
    mid = r.astype(BF16)
    lo = (r - mid.astype(F32)).astype(BF16)
    return hi, mid, lo


def _dot_exact_mask(mask01, x):
    hi, mid, lo = _split3(x)
    return _dot(mask01, hi) + _dot(mask01, mid) + _dot(mask01, lo)


def _log_sigmoid(x):
    return jnp.minimum(x, 0.0) - jnp.log1p(jnp.exp(-jnp.abs(x)))


def _silu(x):
    return x / (1.0 + jnp.exp(-x))


def _log2(n):
    k = n.bit_length() - 1
    assert (1 << k) == n, f"{n} must be a power of two"
    return k


def _params(sem):
    return pltpu.CompilerParams(dimension_semantics=sem, vmem_limit_bytes=VMEM_LIMIT)


def _proj_kernel(x_ref, g_ref, wm_ref, wkvt_ref, ws_ref, wa_ref, ba_ref, bf_ref, *refs, tm, seq_len, names):
    out = dict(zip(names, refs))
    carry_ref = refs[len(names)]
    x = x_ref[...]
    ms = jnp.mean(x * x, axis=-1, keepdims=True)
    h = (x * lax.rsqrt(ms + RMS_EPS) * g_ref[...]).astype(BF16)

    def grp(a, b):
        return _dot(h, wm_ref[:, a:b])

    out["qg"][...] = (grp(_OFF_QG, _OFF_KG) * (GLA_DK ** -0.5)).astype(BF16)
    out["kg"][...] = grp(_OFF_KG, _OFF_VG).astype(BF16)
    out["vg"][...] = grp(_OFF_VG, _OFF_ZG).astype(BF16)
    out["zg"][...] = grp(_OFF_ZG, _OFF_QF).astype(BF16)
    out["zf"][...] = grp(_OFF_ZF, _OFF_END).astype(BF16)

    small = _dot(h, ws_ref[...])
    la = _log_sigmoid(_dot(small.astype(BF16), wa_ref[...]) + ba_ref[...])
    out["la"][...] = la * (1.0 / GLA_GATE_NORM)
    lf = _log_sigmoid(small + bf_ref[...])

    row = lax.broadcasted_iota(jnp.int32, (tm, tm), 0)
    col = lax.broadcasted_iota(jnp.int32, (tm, tm), 1)
    keep = col <= row
    if seq_len < tm:
        sh = _log2(seq_len)
        keep = jnp.logical_and(keep, lax.shift_right_logical(row, sh) == lax.shift_right_logical(col, sh))
    tri = jnp.where(keep, 1.0, 0.0).astype(BF16)
    c = _dot_exact_mask(tri, lf)

    if seq_len >= tm:
        steps = seq_len // tm
        if steps > 1:
            @pl.when(pl.program_id(0) % steps == 0)
            def _():
                carry_ref[...] = jnp.zeros_like(carry_ref)

            c = c + carry_ref[0:1, :]
            carry_ref[0:1, :] = c[tm - 1:tm, :]
        qkvt = _dot_nt(wkvt_ref[...], h)
        out["qft"][0] = (qkvt[0:D_FOX] * (FOX_HD ** -0.5 * _LOG2E)).astype(BF16)
        out["kft"][0] = qkvt[D_FOX:2 * D_FOX]
        out["vft"][0] = qkvt[2 * D_FOX:3 * D_FOX]
        out["vfbt"][0] = qkvt[2 * D_FOX:3 * D_FOX].astype(BF16)
        out["kfb"][...] = grp(_OFF_KF, _OFF_VF).astype(BF16)
        out["lft"][0] = lf.T[0:FOX_H, :]
        out["ct"][0] = c.T[0:FOX_H, :]
    else:
        out["qf"][...] = (grp(_OFF_QF, _OFF_KF) * (FOX_HD ** -0.5)).astype(BF16)
        out["kf"][...] = grp(_OFF_KF, _OFF_VF)
        out["vf"][...] = grp(_OFF_VF, _OFF_ZF)
        out["lf"][...] = lf[:, 0:FOX_H]
        out["cc"][...] = c[:, 0:FOX_H]


def _proj(x2d, g, w_main, w_kvt, w_small, w_a2p, b_a, b_fp, *, seq_len, tm):
    n, d = x2d.shape
    assert n % tm == 0 and (seq_len % tm == 0 or tm % seq_len == 0)
    rows = lambda w: pl.BlockSpec((tm, w), lambda i: (i, 0))
    full = lambda a: pl.BlockSpec(a.shape, lambda i: (0, 0))
    tok = lambda w, dt: (jax.ShapeDtypeStruct((n, w), dt), rows(w))
    outs = {"qg": tok(D_QK, BF16), "kg": tok(D_QK, BF16), "vg": tok(D_GLA, BF16), "zg": tok(D_GLA, BF16),
            "la": tok(D_QK, F32), "zf": tok(D_FOX, BF16)}
    if seq_len >= tm:
        steps = seq_len // tm
        nseq = n // seq_len
        feat = lambda w, dt: (jax.ShapeDtypeStruct((nseq, w, seq_len), dt),
                              pl.BlockSpec((1, w, tm), lambda i: (i // steps, 0, i % steps)))
        outs.update({"qft": feat(D_FOX, BF16), "kfb": tok(D_FOX, BF16), "kft": feat(D_FOX, F32),
                     "vft": feat(D_FOX, F32), "vfbt": feat(D_FOX, BF16),
                     "lft": feat(FOX_H, F32), "ct": feat(FOX_H, F32)})
    else:
        outs.update({"qf": tok(D_FOX, BF16), "kf": tok(D_FOX, F32), "vf": tok(D_FOX, F32),
                     "lf": tok(FOX_H, F32), "cc": tok(FOX_H, F32)})
    names = tuple(outs)
    res = pl.pallas_call(
        functools.partial(_proj_kernel, tm=tm, seq_len=seq_len, names=names),
        grid=(n // tm,),
        in_specs=[rows(d), full(g), full(w_main), full(w_kvt), full(w_small), full(w_a2p), full(b_a), full(b_fp)],
        out_specs=[outs[k][1] for k in names],
        out_shape=[outs[k][0] for k in names],
        scratch_shapes=[pltpu.VMEM((SUBLANE, LANE), F32)],
        compiler_params=_params(("arbitrary",)),
        name="proj",
    )(x2d, g, w_main, w_kvt, w_small, w_a2p, b_a, b_fp)
    return dict(zip(names, res))


def _gla_prepare(q_ref, k_ref, la_ref, rows, chunk):
    la = la_ref[...].reshape(rows, D_QK)
    sh = _log2(chunk)
    row = lax.broadcasted_iota(jnp.int32, (rows, rows), 0)
    col = lax.broadcasted_iota(jnp.int32, (rows, rows), 1)
    same = lax.shift_right_logical(row, sh) == lax.shift_right_logical(col, sh)
    lower = jnp.logical_and(same, col <= row)
    upper = jnp.logical_and(same, col > row)
    cum = _dot_exact_mask(jnp.where(lower, 1.0, 0.0).astype(BF16), la)
    suf = _dot_exact_mask(jnp.where(upper, 1.0, 0.0).astype(BF16), la)
    q = q_ref[...].reshape(rows, D_QK).astype(F32)
    k = k_ref[...].reshape(rows, D_QK).astype(F32)
    qt = (q * jnp.exp(cum)).astype(BF16)
    kt = (k * jnp.exp(-cum)).astype(BF16)
    kp = (k * jnp.exp(suf)).astype(BF16)
    dec = jnp.exp(cum + suf)
    return qt, kt, kp, dec, lower


def _gla_intra(qtp, ktp, vp, lower):
    lane = lax.broadcasted_iota(jnp.int32, (1, 2 * GLA_DK), 1)
    first = lane < GLA_DK
    zero = jnp.zeros_like(qtp)
    a0 = jnp.where(lower, _dot_nt(jnp.where(first, qtp, zero), ktp), 0.0).astype(BF16)
    a1 = jnp.where(lower, _dot_nt(jnp.where(first, zero, qtp), ktp), 0.0).astype(BF16)
    vlane = lax.broadcasted_iota(jnp.int32, (1, 2 * GLA_DV), 1)
    vzero = jnp.zeros_like(vp)
    v0 = jnp.where(vlane < GLA_DV, vp, vzero)
    v1 = jnp.where(vlane < GLA_DV, vzero, vp)
    return _dot(a0, v0) + _dot(a1, v1)


def _pair_blockdiag_mask():
    r = lax.broadcasted_iota(jnp.int32, (2 * GLA_DV, 2 * GLA_DK), 0)
    c = lax.broadcasted_iota(jnp.int32, (2 * GLA_DV, 2 * GLA_DK), 1)
    return lax.shift_right_logical(r, _log2(GLA_DV)) == lax.shift_right_logical(c, _log2(GLA_DK))


def _gla_finish(o, z_ref, gn_ref, rows):
    ys = []
    for h in range(GLA_H):
        oh = o[:, h * GLA_DV:(h + 1) * GLA_DV]
        ms = jnp.mean(oh * oh, axis=-1, keepdims=True)
        ys.append(oh * lax.rsqrt(ms + RMS_EPS) * gn_ref[...])
    y = jnp.concatenate(ys, axis=1) * _silu(z_ref[...].reshape(rows, D_GLA).astype(F32))
    return y.astype(BF16)


def _mix_out(x, yg, of, zf, w_ref, gf_ref):
    yf = (of.astype(F32) * _silu(zf.astype(F32))).astype(BF16)
    d = _dot(yg, w_ref[0:D_GLA, :]) + _dot(yf, w_ref[D_GLA:D_GLA + D_FOX, :])
    xo = x + d
    ms = jnp.mean(xo * xo, axis=-1, keepdims=True)
    return xo * lax.rsqrt(ms + RMS_EPS) * gf_ref[...]


def _state_to_pairs(s_ref, idx, p):
    sa = s_ref[idx + (2 * p,)]
    sb = s_ref[idx + (2 * p + 1,)]
    z = jnp.zeros_like(sa)
    sbd = jnp.concatenate([jnp.concatenate([sa, z], axis=1), jnp.concatenate([z, sb], axis=1)], axis=0)
    return sbd.T


def _pairs_to_state(st, s_ref, idx, p):
    sbd = st.T
    s_ref[idx + (2 * p,)] = sbd[0:GLA_DK, 0:GLA_DV]
    s_ref[idx + (2 * p + 1,)] = sbd[GLA_DK:2 * GLA_DK, GLA_DV:2 * GLA_DV]


def _gla_prompt_kernel(q_ref, k_ref, v_ref, z_ref, la_ref, gn_ref, x_ref, of_ref, zf_ref, w_ref, gf_ref,
                       y_ref, sout_ref, st_ref, *, tb, chunk):
    t = pl.program_id(1)

    @pl.when(t == 0)
    def _():
        st_ref[...] = jnp.zeros_like(st_ref)

    qt, kt, kp, dec, lower = _gla_prepare(q_ref, k_ref, la_ref, tb, chunk)
    v = v_ref[0]
    bd = _pair_blockdiag_mask()
    o_parts = []
    for p in range(GLA_H // 2):
        ql = slice(p * 2 * GLA_DK, (p + 1) * 2 * GLA_DK)
        qtp, ktp, kpp = qt[:, ql], kt[:, ql], kp[:, ql]
        vp = v[:, p * 2 * GLA_DV:(p + 1) * 2 * GLA_DV]
        o_intra = _gla_intra(qtp, ktp, vp, lower)
        s = st_ref[p]
        inter = []
        for c in range(tb // chunk):
            sl = slice(c * chunk, (c + 1) * chunk)
            inter.append(_dot_nt(qtp[sl], s.astype(BF16)))
            upd = _dot_tn(vp[sl], kpp[sl])
            s = s * dec[c * chunk:c * chunk + 1, ql] + jnp.where(bd, upd, 0.0)
        st_ref[p] = s
        o_parts.append(o_intra + jnp.concatenate(inter, axis=0))
    yg = _gla_finish(jnp.concatenate(o_parts, axis=1), z_ref, gn_ref, tb)
    y_ref[0] = _mix_out(x_ref[0], yg, of_ref[0], zf_ref[0], w_ref, gf_ref)

    @pl.when(t == pl.num_programs(1) - 1)
    def _():
        for p in range(GLA_H // 2):
            _pairs_to_state(st_ref[p], sout_ref, (0,), p)


def _gla_prompt(qg, kg, vg, zg, la, gn, x, of, zf, w_out, g_final, *, tb):
    b, t, _ = qg.shape
    d = x.shape[-1]
    assert t % tb == 0 and tb % GLA_CHUNK == 0
    tok = lambda w: pl.BlockSpec((1, tb, w), lambda i, j: (i, j, 0))
    full = lambda a: pl.BlockSpec(a.shape, lambda i, j: (0,) * a.ndim)
    return pl.pallas_call(
        functools.partial(_gla_prompt_kernel, tb=tb, chunk=GLA_CHUNK),
        grid=(b, t // tb),
        in_specs=[tok(D_QK), tok(D_QK), tok(D_GLA), tok(D_GLA), tok(D_QK), full(gn),
                  tok(d), tok(D_FOX), tok(D_FOX), full(w_out), full(g_final)],
        out_specs=[tok(d), pl.BlockSpec((1, GLA_H, GLA_DK, GLA_DV), lambda i, j: (i, 0, 0, 0))],
        out_shape=[jax.ShapeDtypeStruct((b, t, d), F32),
                   jax.ShapeDtypeStruct((b, GLA_H, GLA_DK, GLA_DV), F32)],
        scratch_shapes=[pltpu.VMEM((GLA_H // 2, 2 * GLA_DV, 2 * GLA_DK), F32)],
        compiler_params=_params(("arbitrary", "arbitrary")),
        name="gla_prompt",
    )(qg, kg, vg, zg, la, gn, x, of, zf, w_out, g_final)


def _gla_sample_kernel(q_ref, k_ref, v_ref, z_ref, la_ref, gn_ref, s0_ref, y_ref, sout_ref, *, nb, seq):
    rows = nb * seq
    qt, kt, kp, dec, lower = _gla_prepare(q_ref, k_ref, la_ref, rows, seq)
    v = v_ref[...]
    bd = _pair_blockdiag_mask()
    seq_of_row = lax.shift_right_logical(lax.broadcasted_iota(jnp.int32, (rows, 1), 0), _log2(seq))
    o_parts = []
    for p in range(GLA_H // 2):
        ql = slice(p * 2 * GLA_DK, (p + 1) * 2 * GLA_DK)
        qtp, ktp, kpp = qt[:, ql], kt[:, ql], kp[:, ql]
        vp = v[:, p * 2 * GLA_DV:(p + 1) * 2 * GLA_DV]
        o = _gla_intra(qtp, ktp, vp, lower)
        for i in range(nb):
            mine = seq_of_row == i
            st = _state_to_pairs(s0_ref, (i,), p)
            o = o + jnp.where(mine, _dot_nt(qtp, st.astype(BF16)), 0.0)
            upd = _dot_tn(jnp.where(mine, vp, jnp.zeros_like(vp)), kpp)
            st = st * dec[i * seq:i * seq + 1, ql] + jnp.where(bd, upd, 0.0)
            _pairs_to_state(st, sout_ref, (i,), p)
        o_parts.append(o)
    y_ref[...] = _gla_finish(jnp.concatenate(o_parts, axis=1), z_ref, gn_ref, rows)


def _gla_sample(qg, kg, vg, zg, la, gn, s0, *, seq, nb):
    n = qg.shape[0]
    b = n // seq
    assert b % nb == 0 and (nb * seq) % SUBLANE == 0
    rows = nb * seq
    tok = lambda w: pl.BlockSpec((rows, w), lambda i: (i, 0))
    st = pl.BlockSpec((nb, GLA_H, GLA_DK, GLA_DV), lambda i: (i, 0, 0, 0))
    return pl.pallas_call(
        functools.partial(_gla_sample_kernel, nb=nb, seq=seq),
        grid=(b // nb,),
        in_specs=[tok(D_QK), tok(D_QK), tok(D_GLA), tok(D_GLA), tok(D_QK),
                  pl.BlockSpec((1, GLA_DV), lambda i: (0, 0)), st],
        out_specs=[tok(D_GLA), st],
        out_shape=[jax.ShapeDtypeStruct((n, D_GLA), BF16),
                   jax.ShapeDtypeStruct((b, GLA_H, GLA_DK, GLA_DV), F32)],
        compiler_params=_params(("arbitrary",)),
        name="gla_sample",
    )(qg, kg, vg, zg, la, gn, s0)


def _fox_prompt_kernel(q_ref, k_ref, v_ref, ct_ref, o_ref, kaug_ref, qs_ref, s_ref, m_ref, l_ref, acc_ref,
                       *, seq_len, bq, hp):
    pw = 2 * FOX_HD
    n_split = 3
    tc = min(512, seq_len)
    sub = lax.broadcasted_iota(jnp.int32, (pw, 1), 0)
    first = sub < FOX_HD
    bk = bq // 2
    nq = seq_len // bq
    key = lax.broadcasted_iota(jnp.int32, (bk, bq), 0)
    qry = lax.broadcasted_iota(jnp.int32, (bk, bq), 1)
    diag_masks = (key <= qry, key + bk <= qry)
    sel = [jnp.broadcast_to(jnp.where(jnp.logical_and(sub >= n_split * hl, sub < n_split * (hl + 1)),
                                      1.0, 0.0).astype(BF16), (pw, bq)) for hl in range(2)]

    for pi in range(hp):
        kaug_ref[pi, :, 0:pw] = k_ref[0, :, pi * pw:(pi + 1) * pw]
        parts = []
        for hl in range(2):
            parts.extend(p.astype(F32) for p in _split3(ct_ref[0, pi, hl:hl + 1, :] * (-_LOG2E)))
        parts.append(jnp.zeros((pw - 2 * n_split, seq_len), F32))
        bias_rows = jnp.concatenate(parts, axis=0)
        for t0 in range(0, seq_len, tc):
            kaug_ref[pi, t0:t0 + tc, pw:2 * pw] = bias_rows[:, t0:t0 + tc].T.astype(BF16)

    def load_queries(i, buf):
        q0 = pl.multiple_of(i * bq, bq)
        for pi in range(hp):
            qt = q_ref[0, pi * pw:(pi + 1) * pw, pl.ds(q0, bq)]
            zero = jnp.zeros_like(qt)
            qs_ref[buf, 2 * pi] = jnp.concatenate([jnp.where(first, qt, zero), sel[0]], axis=0)
            qs_ref[buf, 2 * pi + 1] = jnp.concatenate([jnp.where(first, zero, qt), sel[1]], axis=0)

    def scores(j, slot, buf):
        ks = pl.multiple_of(j * bk, bk)
        for h in range(2 * hp):
            s_ref[slot, h] = _dot(kaug_ref[h // 2, pl.ds(ks, bk), :], qs_ref[buf, h])

    def consume(j, slot, mask=None):
        ks = pl.multiple_of(j * bk, bk)
        for h in range(2 * hp):
            s = s_ref[slot, h]
            if mask is not None:
                s = jnp.where(mask, s, -jnp.inf)
            m = m_ref[h]
            mn = jnp.maximum(m, jnp.max(s, axis=0, keepdims=True))
            alpha = jnp.exp2(m - mn)
            pr = jnp.exp2(s - mn)
            m_ref[h] = mn
            l_ref[h] = l_ref[h] * alpha + jnp.sum(pr, axis=0, keepdims=True)
            vb = v_ref[0, h * FOX_HD:(h + 1) * FOX_HD, pl.ds(ks, bk)]
            acc_ref[h] = acc_ref[h] * alpha + _dot(vb, pr.astype(BF16))

    def q_block(i, buf):
        m_ref[...] = jnp.full_like(m_ref, -jnp.inf)
        l_ref[...] = jnp.zeros_like(l_ref)
        acc_ref[...] = jnp.zeros_like(acc_ref)

        def kv_pair(jj, _):
            j = 2 * jj
            scores(j + 1, 1, buf)
            consume(j, 0)
            scores(j + 2, 0, buf)
            consume(j + 1, 1)
            return 0

        lax.fori_loop(0, i, kv_pair, 0)
        scores(2 * i + 1, 1, buf)
        consume(2 * i, 0, diag_masks[0])
        load_queries(jnp.minimum(i + 1, nq - 1), 1 - buf)
        scores(0, 0, 1 - buf)
        consume(2 * i + 1, 1, diag_masks[1])
        q0 = pl.multiple_of(i * bq, bq)
        for pi in range(hp):
            pair_t = jnp.concatenate([acc_ref[2 * pi] / l_ref[2 * pi],
                                      acc_ref[2 * pi + 1] / l_ref[2 * pi + 1]], axis=0)
            o_ref[0, pl.ds(q0, bq), pi * pw:(pi + 1) * pw] = pair_t.T.astype(BF16)

    def q_pair(u, _):
        q_block(2 * u, 0)
        q_block(2 * u + 1, 1)
        return 0

    load_queries(0, 0)
    scores(0, 0, 0)
    lax.fori_loop(0, nq // 2, q_pair, 0)


def _fox_prompt(qft, kf, vft, ct, *, bq, hp):
    b, _, t = qft.shape
    pairs = FOX_H // 2
    assert t % (2 * bq) == 0 and pairs % hp == 0
    pw = 2 * FOX_HD
    tok = pl.BlockSpec((1, t, hp * pw), lambda i, p: (i, 0, p))
    feat = pl.BlockSpec((1, hp * pw, t), lambda i, p: (i, p, 0))
    return pl.pallas_call(
        functools.partial(_fox_prompt_kernel, seq_len=t, bq=bq, hp=hp),
        grid=(b, pairs // hp),
        in_specs=[feat, tok, feat, pl.BlockSpec((1, hp, 2, t), lambda i, p: (i, p, 0, 0))],
        out_specs=tok,
        out_shape=jax.ShapeDtypeStruct((b, t, D_FOX), BF16),
        scratch_shapes=[pltpu.VMEM((hp, t, 2 * pw), BF16),
                        pltpu.VMEM((2, 2 * hp, 2 * pw, bq), BF16),
                        pltpu.VMEM((2, 2 * hp, bq // 2, bq), F32),
                        pltpu.VMEM((2 * hp, 1, bq), F32),
                        pltpu.VMEM((2 * hp, 1, bq), F32),
                        pltpu.VMEM((2 * hp, FOX_HD, bq), F32)],
        compiler_params=_params(("arbitrary", "arbitrary")),
        name="fox_prompt",
    )(qft, kf, vft, ct)


def _lf_prep_kernel(lf_ref, m_ref, o_ref):
    hi, mid, lo = _split3(lf_ref[...])
    o_ref[...] = _dot(hi, m_ref[...]) + _dot(mid, m_ref[...]) + _dot(lo, m_ref[...])


def _lf_prep(lf_rows, *, tp):
    n, page = lf_rows.shape
    src = lax.broadcasted_iota(jnp.int32, (page, 2 * page), 0)
    dst = lax.broadcasted_iota(jnp.int32, (page, 2 * page), 1)
    mask = jnp.where(jnp.logical_or(dst >= page, src > dst), 1.0, 0.0).astype(BF16)
    assert n % tp == 0
    return pl.pallas_call(
        _lf_prep_kernel,
        grid=(n // tp,),
        in_specs=[pl.BlockSpec((tp, page), lambda i: (i, 0)), pl.BlockSpec(mask.shape, lambda i: (0, 0))],
        out_specs=pl.BlockSpec((tp, 2 * page), lambda i: (i, 0)),
        out_shape=jax.ShapeDtypeStruct((n, 2 * page), F32),
        compiler_params=_params(("arbitrary",)),
        name="lf_prep",
    )(lf_rows, mask)


def _fox_sample_kernel(pt_ref, q_ref, kn_ref, vn_ref, cc_ref, *rest, pp, seq, page):
    k_refs, v_refs, e_refs = rest[:pp], rest[pp:2 * pp], rest[2 * pp:3 * pp]
    o_ref, qbd_ref, m_ref, l_ref, acc_ref, car_ref = rest[3 * pp:]
    del pt_ref
    c = pl.program_id(1)
    rows = seq * FOX_H
    sub = lax.broadcasted_iota(jnp.int32, (FOX_H, D_FOX), 0)
    lane = lax.broadcasted_iota(jnp.int32, (FOX_H, D_FOX), 1)
    own = lax.shift_right_logical(lane, _log2(FOX_HD)) == sub

    @pl.when(c == 0)
    def _():
        q = q_ref[0]
        qbd_ref[...] = jnp.concatenate([jnp.where(own, q[i:i + 1, :], 0.0) for i in range(seq)], axis=0)
        m_ref[...] = jnp.full_like(m_ref, -jnp.inf)
        l_ref[...] = jnp.zeros_like(l_ref)
        acc_ref[...] = jnp.zeros_like(acc_ref)
        car_ref[...] = jnp.zeros_like(car_ref)

    qbd = qbd_ref[...]
    carry = car_ref[...]
    bias = [None] * pp
    for i in reversed(range(pp)):
        e = e_refs[i][0]
        bias[i] = jnp.concatenate([e[:, 0:page] + carry] * seq, axis=0)
        carry = carry + e[:, page:2 * page]
    car_ref[...] = carry
    kcat = jnp.concatenate([k_refs[i][0] for i in range(pp)], axis=1)
    s = _dot(qbd, kcat) + jnp.concatenate(bias, axis=1)
    m_old = m_ref[...]
    mn = jnp.maximum(m_old, jnp.max(s, axis=-1, keepdims=True))
    alpha = jnp.exp(m_old - mn)
    pr = jnp.exp(s - mn)
    l_ref[...] = l_ref[...] * alpha + jnp.sum(pr, axis=-1, keepdims=True)
    vcat = jnp.concatenate([v_refs[i][0] for i in range(pp)], axis=1)
    acc_ref[...] = acc_ref[...] * alpha + _dot_nt(pr, vcat)
    m_ref[...] = mn

    @pl.when(c == pl.num_programs(1) - 1)
    def _():
        kn = kn_ref[0]
        vn = vn_ref[0]
        cc = cc_ref[0]
        eye = (lax.broadcasted_iota(jnp.int32, (FOX_H, FOX_H), 0)
               == lax.broadcasted_iota(jnp.int32, (FOX_H, FOX_H), 1))
        q_of_row = lax.shift_right_logical(lax.broadcasted_iota(jnp.int32, (rows, 1), 0), _log2(FOX_H))
        s_new = []
        for j in range(seq):
            cj = jnp.sum(jnp.where(eye, cc[j:j + 1, :], 0.0), axis=-1, keepdims=True)
            sj = jnp.sum(qbd * kn[j:j + 1, :], axis=-1, keepdims=True) - jnp.concatenate([cj] * seq, axis=0)
            s_new.append(jnp.where(q_of_row >= j, sj, -jnp.inf))
        m_prev = m_ref[...]
        m_fin = m_prev
        for sj in s_new:
            m_fin = jnp.maximum(m_fin, sj)
        a_fin = jnp.exp(m_prev - m_fin)
        l_fin = l_ref[...] * a_fin
        acc = acc_ref[...] * a_fin
        for j, sj in enumerate(s_new):
            pj = jnp.exp(sj - m_fin)
            l_fin = l_fin + pj
            acc = acc + pj * vn[j:j + 1, :]
        out = acc / l_fin
        o_ref[0] = jnp.concatenate(
            [jnp.sum(jnp.where(own, out[i * FOX_H:(i + 1) * FOX_H, :], 0.0), axis=0, keepdims=True)
             for i in range(seq)], axis=0)


def _fox_sample(page_table, qf, kn, vn, cc, cache_kt, cache_vt, e_pages, *, pp):
    b, seq, _ = qf.shape
    n_pages = page_table.shape[1]
    assert n_pages % pp == 0
    page = cache_kt.shape[2]
    n_chunks = n_pages // pp
    tok = lambda w: pl.BlockSpec((1, seq, w), lambda i, c, pt: (i, 0, 0))

    def paged(shape, k):
        return pl.BlockSpec((1,) + shape, lambda i, c, pt: (pt[i, (n_chunks - 1 - c) * pp + k], 0, 0))

    rows = seq * FOX_H
    grid_spec = pltpu.PrefetchScalarGridSpec(
        num_scalar_prefetch=1,
        grid=(b, n_chunks),
        in_specs=[tok(D_FOX), tok(D_FOX), tok(D_FOX), tok(FOX_H)]
        + [paged((D_FOX, page), k) for k in range(pp)]
        + [paged((D_FOX, page), k) for k in range(pp)]
        + [paged((FOX_H, 2 * page), k) for k in range(pp)],
        out_specs=tok(D_FOX),
        scratch_shapes=[pltpu.VMEM((rows, D_FOX), F32), pltpu.VMEM((rows, 1), F32), pltpu.VMEM((rows, 1), F32),
                        pltpu.VMEM((rows, D_FOX), F32), pltpu.VMEM((FOX_H, page), F32)],
    )
    return pl.pallas_call(
        functools.partial(_fox_sample_kernel, pp=pp, seq=seq, page=page),
        grid_spec=grid_spec,
        out_shape=jax.ShapeDtypeStruct((b, seq, D_FOX), F32),
        compiler_params=_params(("arbitrary", "arbitrary")),
        name="fox_sample",
    )(page_table, qf, kn, vn, cc, *([cache_kt] * pp), *([cache_vt] * pp), *([e_pages] * pp))


def _out_kernel(x_ref, yg_ref, of_ref, zf_ref, w_ref, gf_ref, y_ref):
    y_ref[...] = _mix_out(x_ref[...], yg_ref[...], of_ref[...], zf_ref[...], w_ref, gf_ref)


def _out(x2d, yg, of, zf, w_out, g_final, *, tm):
    n, d = x2d.shape
    rows = lambda w: pl.BlockSpec((tm, w), lambda i: (i, 0))
    full = lambda a: pl.BlockSpec(a.shape, lambda i: (0, 0))
    return pl.pallas_call(
        _out_kernel,
        grid=(n // tm,),
        in_specs=[rows(d), rows(D_GLA), rows(D_FOX), rows(D_FOX), full(w_out), full(g_final)],
        out_specs=rows(d),
        out_shape=jax.ShapeDtypeStruct((n, d), F32),
        compiler_params=_params(("arbitrary",)),
        name="out",
    )(x2d, yg, of, zf, w_out, g_final)


def _regroup_weights(w_in, w_a2, b_a, b_f):
    o_alr = 2 * D_QK + 2 * D_GLA
    o_qf = o_alr + GLA_RANK
    o_fl = o_qf + 4 * D_FOX
    d = w_in.shape[0]
    w_main = jnp.concatenate([w_in[:, :o_alr], w_in[:, o_qf:o_fl]], axis=1).astype(BF16)
    w_kvt = w_in[:, o_qf:o_qf + 3 * D_FOX].T.astype(BF16)
    pad = LANE - FOX_H - GLA_RANK
    w_small = jnp.concatenate([w_in[:, o_fl:o_fl + FOX_H], w_in[:, o_alr:o_qf], jnp.zeros((d, pad), F32)],
                              axis=1).astype(BF16)
    w_a2p = jnp.concatenate([jnp.zeros((FOX_H, D_QK), F32), w_a2, jnp.zeros((pad, D_QK), F32)],
                            axis=0).astype(BF16)
    b_fp = jnp.concatenate([b_f, jnp.zeros((LANE - FOX_H,), F32)])[None, :]
    return w_main, w_kvt, w_small, w_a2p, b_a[None, :], b_fp


def kernel(x_prompt, x_sample, cache_k, cache_v, cache_logf, state_gla, page_table,
           g_norm, w_in, w_a2, b_a, b_f, gla_norm, w_out, g_final):
    depth = w_in.shape[0]
    assert depth == 1, "single-layer step"
    bp, tp, d = x_prompt.shape
    bs, ts, _ = x_sample.shape
    n_pool, page = cache_k.shape[1], cache_k.shape[2]
    l = 0
    weights = _regroup_weights(w_in[l], w_a2[l], b_a[l], b_f[l])
    g = g_norm[l][None, :]
    gn = gla_norm[l][None, :]
    w_o = w_out[l].astype(BF16)
    gf = g_final[None, :]

    xp = x_prompt.reshape(bp * tp, d)
    pr = _proj(xp, g, *weights, seq_len=tp, tm=min(512, tp))
    r3 = lambda a: a.reshape(bp, tp, a.shape[-1])
    of = _fox_prompt(pr["qft"], r3(pr["kfb"]), pr["vfbt"], pr["ct"].reshape(bp, FOX_H // 2, 2, tp),
                     bq=min(512, tp // 2), hp=2)
    y_prompt, s_prompt = _gla_prompt(r3(pr["qg"]), r3(pr["kg"]), r3(pr["vg"]), r3(pr["zg"]), r3(pr["la"]), gn,
                                     x_prompt, of, r3(pr["zf"]), w_o, gf, tb=min(256, tp))
    heads_last = lambda a: a.reshape(bp, FOX_H, FOX_HD, tp).transpose(0, 3, 1, 2)[None]
    k_prompt, v_prompt = heads_last(pr["kft"]), heads_last(pr["vft"])
    lf_prompt = pr["lft"].transpose(0, 2, 1)[None]

    xs = x_sample.reshape(bs * ts, d)
    sm = _proj(xs, g, *weights, seq_len=ts, tm=min(512, bs * ts))
    yg_s, s_sample = _gla_sample(sm["qg"], sm["kg"], sm["vg"], sm["zg"], sm["la"], gn, state_gla[l], seq=ts, nb=8)
    ckt = cache_k[l].transpose(0, 2, 3, 1).reshape(n_pool, D_FOX, page)
    cvt = cache_v[l].transpose(0, 2, 3, 1).reshape(n_pool, D_FOX, page)
    clf = cache_logf[l].transpose(0, 2, 1).reshape(n_pool * FOX_H, page)
    e_pages = _lf_prep(clf, tp=math.gcd(n_pool * FOX_H, 2048)).reshape(n_pool, FOX_H, 2 * page)
    s3 = lambda a: a.reshape(bs, ts, a.shape[-1])
    of_s = _fox_sample(page_table, s3(sm["qf"].astype(F32)), s3(sm["kf"]), s3(sm["vf"]), s3(sm["cc"]),
                       ckt, cvt, e_pages, pp=math.gcd(page_table.shape[1], 32))
    y_sample = _out(xs, yg_s, of_s.reshape(bs * ts, D_FOX), sm["zf"], w_o, gf, tm=min(512, bs * ts))

    return (y_prompt, y_sample.reshape(bs, ts, d),
            k_prompt, v_prompt, lf_prompt, s_prompt[None],
            sm["kf"].reshape(1, bs, ts, FOX_H, FOX_HD), sm["vf"].reshape(1, bs, ts, FOX_H, FOX_HD),
            sm["lf"].reshape(1, bs, ts, FOX_H), s_sample[None])
```

```python
import functools
import math

import jax
import jax.numpy as jnp
from jax import lax
from jax.experimental import pallas as pl
from jax.experimental.pallas import tpu as pltpu

F32 = jnp.float32
BF16 = jnp.bfloat16

GLA_H = 4
GLA_DK = 64
GLA_DV = 128
GLA_RANK = 16
GLA_GATE_NORM = 16.0
GLA_CHUNK = 32
FOX_H = 8
FOX_HD = 64
RMS_EPS = 1e-6
_LOG2E = 1.4426950408889634

D_QK = GLA_H * GLA_DK
D_GLA = GLA_H * GLA_DV
D_FOX = FOX_H * FOX_HD
LANE = 128
SUBLANE = 8
VMEM_LIMIT = 56 * 1024 * 1024

_OFF_QG, _OFF_KG, _OFF_VG, _OFF_ZG = 0, 256, 512, 1024
_OFF_QF, _OFF_KF, _OFF_VF, _OFF_ZF, _OFF_END = 1536, 2048, 2560, 3072, 3584


def _dot(a, b):
    return jnp.dot(a, b, preferred_element_type=F32)


def _dot_nt(a, b):
    return lax.dot_general(a, b, (((1,), (1,)), ((), ())), preferred_element_type=F32)


def _dot_tn(a, b):
    return lax.dot_general(a, b, (((0,), (0,)), ((), ())), preferred_element_type=F32)


def _split3(x):
    hi = x.astype(BF16)
    r = x - hi.astype(F32)
    mid = r.astype(BF16)
    lo = (r - mid.astype(F32)).astype(BF16)
    return hi, mid, lo


def _dot_exact_mask(mask01, x):
    hi, mid, lo = _split3(x)
    return _dot(mask01, hi) + _dot(mask01, mid) + _dot(mask01, lo)


def _log_sigmoid(x):
    return jnp.minimum(x, 0.0) - jnp.log1p(jnp.exp(-jnp.abs(x)))


def _silu(x):
    return x / (1.0 + jnp.exp(-x))


def _log2(n):
    k = n.bit_length() - 1
    assert (1 << k) == n, f"{n} must be a power of two"
    return k


def _params(sem):
    return pltpu.CompilerParams(dimension_semantics=sem, vmem_limit_bytes=VMEM_LIMIT)


def _proj_kernel(x_ref, g_ref, wm_ref, wkvt_ref, ws_ref, wa_ref, ba_ref, bf_ref, *refs, tm, seq_len, names):
    out = dict(zip(names, refs))
    carry_ref = refs[len(names)]
    x = x_ref[...]
    ms = jnp.mean(x * x, axis=-1, keepdims=True)
    h = (x * lax.rsqrt(ms + RMS_EPS) * g_ref[...]).astype(BF16)

    def grp(a, b):
        return _dot(h, wm_ref[:, a:b])

    out["qg"][...] = (grp(_OFF_QG, _OFF_KG) * (GLA_DK ** -0.5)).astype(BF16)
    out["kg"][...] = grp(_OFF_KG, _OFF_VG).astype(BF16)
    out["vg"][...] = grp(_OFF_VG, _OFF_ZG).astype(BF16)
    out["zg"][...] = grp(_OFF_ZG, _OFF_QF).astype(BF16)
    out["zf"][...] = grp(_OFF_ZF, _OFF_END).astype(BF16)

    small = _dot(h, ws_ref[...])
    la = _log_sigmoid(_dot(small.astype(BF16), wa_ref[...]) + ba_ref[...])
    out["la"][...] = la * (1.0 / GLA_GATE_NORM)
    lf = _log_sigmoid(small + bf_ref[...])

    row = lax.broadcasted_iota(jnp.int32, (tm, tm), 0)
    col = lax.broadcasted_iota(jnp.int32, (tm, tm), 1)
    keep = col <= row
    if seq_len < tm:
        sh = _log2(seq_len)
        keep = jnp.logical_and(keep, lax.shift_right_logical(row, sh) == lax.shift_right_logical(col, sh))
    tri = jnp.where(keep, 1.0, 0.0).astype(BF16)
    c = _dot_exact_mask(tri, lf)

    if seq_len >= tm:
        steps = seq_len // tm
        if steps > 1:
            @pl.when(pl.program_id(0) % steps == 0)
            def _():
                carry_ref[...] = jnp.zeros_like(carry_ref)

            c = c + carry_ref[0:1, :]
            carry_ref[0:1, :] = c[tm - 1:tm, :]
        qkvt = _dot_nt(wkvt_ref[...], h)
        out["qft"][0] = (qkvt[0:D_FOX] * (FOX_HD ** -0.5 * _LOG2E)).astype(BF16)
        out["kft"][0] = qkvt[D_FOX:2 * D_FOX]
        out["vft"][0] = qkvt[2 * D_FOX:3 * D_FOX]
        out["vfbt"][0] = qkvt[2 * D_FOX:3 * D_FOX].astype(BF16)
        out["kfb"][...] = grp(_OFF_KF, _OFF_VF).astype(BF16)
        out["lft"][0] = lf.T[0:FOX_H, :]
        out["ct"][0] = c.T[0:FOX_H, :]
    else:
        out["qf"][...] = (grp(_OFF_QF, _OFF_KF) * (FOX_HD ** -0.5)).astype(BF16)
        out["kf"][...] = grp(_OFF_KF, _OFF_VF)
        out["vf"][...] = grp(_OFF_VF, _OFF_ZF)
        out["lf"][...] = lf[:, 0:FOX_H]
        out["cc"][...] = c[:, 0:FOX_H]


def _proj(x2d, g, w_main, w_kvt, w_small, w_a2p, b_a, b_fp, *, seq_len, tm):
    n, d = x2d.shape
    assert n % tm == 0 and (seq_len % tm == 0 or tm % seq_len == 0)
    rows = lambda w: pl.BlockSpec((tm, w), lambda i: (i, 0))
    full = lambda a: pl.BlockSpec(a.shape, lambda i: (0, 0))
    tok = lambda w, dt: (jax.ShapeDtypeStruct((n, w), dt), rows(w))
    outs = {"qg": tok(D_QK, BF16), "kg": tok(D_QK, BF16), "vg": tok(D_GLA, BF16), "zg": tok(D_GLA, BF16),
            "la": tok(D_QK, F32), "zf": tok(D_FOX, BF16)}
    if seq_len >= tm:
        steps = seq_len // tm
        nseq = n // seq_len
        feat = lambda w, dt: (jax.ShapeDtypeStruct((nseq, w, seq_len), dt),
                              pl.BlockSpec((1, w, tm), lambda i: (i // steps, 0, i % steps)))
        outs.update({"qft": feat(D_FOX, BF16), "kfb": tok(D_FOX, BF16), "kft": feat(D_FOX, F32),
                     "vft": feat(D_FOX, F32), "vfbt": feat(D_FOX, BF16),
                     "lft": feat(FOX_H, F32), "ct": feat(FOX_H, F32)})
    else:
        outs.update({"qf": tok(D_FOX, BF16), "kf": tok(D_FOX, F32), "vf": tok(D_FOX, F32),
                     "lf": tok(FOX_H, F32), "cc": tok(FOX_H, F32)})
    names = tuple(outs)
    res = pl.pallas_call(
        functools.partial(_proj_kernel, tm=tm, seq_len=seq_len, names=names),
        grid=(n // tm,),
        in_specs=[rows(d), full(g), full(w_main), full(w_kvt), full(w_small), full(w_a2p), full(b_a), full(b_fp)],
        out_specs=[outs[k][1] for k in names],
        out_shape=[outs[k][0] for k in names],
        scratch_shapes=[pltpu.VMEM((SUBLANE, LANE), F32)],
        compiler_params=_params(("arbitrary",)),
        name="proj",
    )(x2d, g, w_main, w_kvt, w_small, w_a2p, b_a, b_fp)
    return dict(zip(names, res))


def _gla_prepare(q_ref, k_ref, la_ref, rows, chunk):
    la = la_ref[...].reshape(rows, D_QK)
    sh = _log2(chunk)
    row = lax.broadcasted_iota(jnp.int32, (rows, rows), 0)
    col = lax.broadcasted_iota(jnp.int32, (rows, rows), 1)
    same = lax.shift_right_logical(row, sh) == lax.shift_right_logical(col, sh)
    lower = jnp.logical_and(same, col <= row)
    upper = jnp.logical_and(same, col > row)
    cum = _dot_exact_mask(jnp.where(lower, 1.0, 0.0).astype(BF16), la)
    suf = _dot_exact_mask(jnp.where(upper, 1.0, 0.0).astype(BF16), la)
    q = q_ref[...].reshape(rows, D_QK).astype(F32)
    k = k_ref[...].reshape(rows, D_QK).astype(F32)
    qt = (q * jnp.exp(cum)).astype(BF16)
    kt = (k * jnp.exp(-cum)).astype(BF16)
    kp = (k * jnp.exp(suf)).astype(BF16)
    dec = jnp.exp(cum + suf)
    return qt, kt, kp, dec, lower


def _gla_intra(qtp, ktp, vp, lower):
    lane = lax.broadcasted_iota(jnp.int32, (1, 2 * GLA_DK), 1)
    first = lane < GLA_DK
    zero = jnp.zeros_like(qtp)
    a0 = jnp.where(lower, _dot_nt(jnp.where(first, qtp, zero), ktp), 0.0).astype(BF16)
    a1 = jnp.where(lower, _dot_nt(jnp.where(first, zero, qtp), ktp), 0.0).astype(BF16)
    vlane = lax.broadcasted_iota(jnp.int32, (1, 2 * GLA_DV), 1)
    vzero = jnp.zeros_like(vp)
    v0 = jnp.where(vlane < GLA_DV, vp, vzero)
    v1 = jnp.where(vlane < GLA_DV, vzero, vp)
    return _dot(a0, v0) + _dot(a1, v1)


def _pair_blockdiag_mask():
    r = lax.broadcasted_iota(jnp.int32, (2 * GLA_DV, 2 * GLA_DK), 0)
    c = lax.broadcasted_iota(jnp.int32, (2 * GLA_DV, 2 * GLA_DK), 1)
    return lax.shift_right_logical(r, _log2(GLA_DV)) == lax.shift_right_logical(c, _log2(GLA_DK))


def _gla_finish(o, z_ref, gn_ref, rows):
    ys = []
    for h in range(GLA_H):
        oh = o[:, h * GLA_DV:(h + 1) * GLA_DV]
        ms = jnp.mean(oh * oh, axis=-1, keepdims=True)
        ys.append(oh * lax.rsqrt(ms + RMS_EPS) * gn_ref[...])
    y = jnp.concatenate(ys, axis=1) * _silu(z_ref[...].reshape(rows, D_GLA).astype(F32))
    return y.astype(BF16)


def _mix_out(x, yg, of, zf, w_ref, gf_ref):
    yf = (of.astype(F32) * _silu(zf.astype(F32))).astype(BF16)
    d = _dot(yg, w_ref[0:D_GLA, :]) + _dot(yf, w_ref[D_GLA:D_GLA + D_FOX, :])
    xo = x + d
    ms = jnp.mean(xo * xo, axis=-1, keepdims=True)
    return xo * lax.rsqrt(ms + RMS_EPS) * gf_ref[...]


def _state_to_pairs(s_ref, idx, p):
    sa = s_ref[idx + (2 * p,)]
    sb = s_ref[idx + (2 * p + 1,)]
    z = jnp.zeros_like(sa)
    sbd = jnp.concatenate([jnp.concatenate([sa, z], axis=1), jnp.concatenate([z, sb], axis=1)], axis=0)
    return sbd.T


def _pairs_to_state(st, s_ref, idx, p):
    sbd = st.T
    s_ref[idx + (2 * p,)] = sbd[0:GLA_DK, 0:GLA_DV]
    s_ref[idx + (2 * p + 1,)] = sbd[GLA_DK:2 * GLA_DK, GLA_DV:2 * GLA_DV]


def _gla_prompt_kernel(q_ref, k_ref, v_ref, z_ref, la_ref, gn_ref, x_ref, of_ref, zf_ref, w_ref, gf_ref,
                       y_ref, sout_ref, st_ref, *, tb, chunk):
    t = pl.program_id(1)

    @pl.when(t == 0)
    def _():
        st_ref[...] = jnp.zeros_like(st_ref)

    qt, kt, kp, dec, lower = _gla_prepare(q_ref, k_ref, la_ref, tb, chunk)
    v = v_ref[0]
    bd = _pair_blockdiag_mask()
    o_parts = []
    for p in range(GLA_H // 2):
        ql = slice(p * 2 * GLA_DK, (p + 1) * 2 * GLA_DK)
        qtp, ktp, kpp = qt[:, ql], kt[:, ql], kp[:, ql]
        vp = v[:, p * 2 * GLA_DV:(p + 1) * 2 * GLA_DV]
        o_intra = _gla_intra(qtp, ktp, vp, lower)
        s = st_ref[p]
        inter = []
        for c in range(tb // chunk):
            sl = slice(c * chunk, (c + 1) * chunk)
            inter.append(_dot_nt(qtp[sl], s.astype(BF16)))
            upd = _dot_tn(vp[sl], kpp[sl])
            s = s * dec[c * chunk:c * chunk + 1, ql] + jnp.where(bd, upd, 0.0)
        st_ref[p] = s
        o_parts.append(o_intra + jnp.concatenate(inter, axis=0))
    yg = _gla_finish(jnp.concatenate(o_parts, axis=1), z_ref, gn_ref, tb)
    y_ref[0] = _mix_out(x_ref[0], yg, of_ref[0], zf_ref[0], w_ref, gf_ref)

    @pl.when(t == pl.num_programs(1) - 1)
    def _():
        for p in range(GLA_H // 2):
            _pairs_to_state(st_ref[p], sout_ref, (0,), p)


def _gla_prompt(qg, kg, vg, zg, la, gn, x, of, zf, w_out, g_final, *, tb):
    b, t, _ = qg.shape
    d = x.shape[-1]
    assert t % tb == 0 and tb % GLA_CHUNK == 0
    tok = lambda w: pl.BlockSpec((1, tb, w), lambda i, j: (i, j, 0))
    full = lambda a: pl.BlockSpec(a.shape, lambda i, j: (0,) * a.ndim)
    return pl.pallas_call(
        functools.partial(_gla_prompt_kernel, tb=tb, chunk=GLA_CHUNK),
        grid=(b, t // tb),
        in_specs=[tok(D_QK), tok(D_QK), tok(D_GLA), tok(D_GLA), tok(D_QK), full(gn),
                  tok(d), tok(D_FOX), tok(D_FOX), full(w_out), full(g_final)],
        out_specs=[tok(d), pl.BlockSpec((1, GLA_H, GLA_DK, GLA_DV), lambda i, j: (i, 0, 0, 0))],
        out_shape=[jax.ShapeDtypeStruct((b, t, d), F32),
                   jax.ShapeDtypeStruct((b, GLA_H, GLA_DK, GLA_DV), F32)],
        scratch_shapes=[pltpu.VMEM((GLA_H // 2, 2 * GLA_DV, 2 * GLA_DK), F32)],
        compiler_params=_params(("arbitrary", "arbitrary")),
        name="gla_prompt",
    )(qg, kg, vg, zg, la, gn, x, of, zf, w_out, g_final)


def _gla_sample_kernel(q_ref, k_ref, v_ref, z_ref, la_ref, gn_ref, s0_ref, y_ref, sout_ref, *, nb, seq):
    rows = nb * seq
    qt, kt, kp, dec, lower = _gla_prepare(q_ref, k_ref, la_ref, rows, seq)
    v = v_ref[...]
    bd = _pair_blockdiag_mask()
    seq_of_row = lax.shift_right_logical(lax.broadcasted_iota(jnp.int32, (rows, 1), 0), _log2(seq))
    o_parts = []
    for p in range(GLA_H // 2):
        ql = slice(p * 2 * GLA_DK, (p + 1) * 2 * GLA_DK)
        qtp, ktp, kpp = qt[:, ql], kt[:, ql], kp[:, ql]
        vp = v[:, p * 2 * GLA_DV:(p + 1) * 2 * GLA_DV]
        o = _gla_intra(qtp, ktp, vp, lower)
        for i in range(nb):
            mine = seq_of_row == i
            st = _state_to_pairs(s0_ref, (i,), p)
            o = o + jnp.where(mine, _dot_nt(qtp, st.astype(BF16)), 0.0)
            upd = _dot_tn(jnp.where(mine, vp, jnp.zeros_like(vp)), kpp)
            st = st * dec[i * seq:i * seq + 1, ql] + jnp.where(bd, upd, 0.0)
            _pairs_to_state(st, sout_ref, (i,), p)
        o_parts.append(o)
    y_ref[...] = _gla_finish(jnp.concatenate(o_parts, axis=1), z_ref, gn_ref, rows)


def _gla_sample(qg, kg, vg, zg, la, gn, s0, *, seq, nb):
    n = qg.shape[0]
    b = n // seq
    assert b % nb == 0 and (nb * seq) % SUBLANE == 0
    rows = nb * seq
    tok = lambda w: pl.BlockSpec((rows, w), lambda i: (i, 0))
    st = pl.BlockSpec((nb, GLA_H, GLA_DK, GLA_DV), lambda i: (i, 0, 0, 0))
    return pl.pallas_call(
        functools.partial(_gla_sample_kernel, nb=nb, seq=seq),
        grid=(b // nb,),
        in_specs=[tok(D_QK), tok(D_QK), tok(D_GLA), tok(D_GLA), tok(D_QK),
                  pl.BlockSpec((1, GLA_DV), lambda i: (0, 0)), st],
        out_specs=[tok(D_GLA), st],
        out_shape=[jax.ShapeDtypeStruct((n, D_GLA), BF16),
                   jax.ShapeDtypeStruct((b, GLA_H, GLA_DK, GLA_DV), F32)],
        compiler_params=_params(("arbitrary",)),
        name="gla_sample",
    )(qg, kg, vg, zg, la, gn, s0)


def _fox_prompt_kernel(q_ref, k_ref, v_ref, ct_ref, o_ref, kaug_ref, qs_ref, s_ref, m_ref, l_ref, acc_ref,
                       *, seq_len, bq, hp):
    pw = 2 * FOX_HD
    n_split = 3
    tc = min(512, seq_len)
    sub = lax.broadcasted_iota(jnp.int32, (pw, 1), 0)
    first = sub < FOX_HD
    bk = bq // 2
    nq = seq_len // bq
    key = lax.broadcasted_iota(jnp.int32, (bk, bq), 0)
    qry = lax.broadcasted_iota(jnp.int32, (bk, bq), 1)
    diag_masks = (key <= qry, key + bk <= qry)
    sel = [jnp.broadcast_to(jnp.where(jnp.logical_and(sub >= n_split * hl, sub < n_split * (hl + 1)),
                                      1.0, 0.0).astype(BF16), (pw, bq)) for hl in range(2)]

    for pi in range(hp):
        kaug_ref[pi, :, 0:pw] = k_ref[0, :, pi * pw:(pi + 1) * pw]
        parts = []
        for hl in range(2):
            parts.extend(p.astype(F32) for p in _split3(ct_ref[0, pi, hl:hl + 1, :] * (-_LOG2E)))
        parts.append(jnp.zeros((pw - 2 * n_split, seq_len), F32))
        bias_rows = jnp.concatenate(parts, axis=0)
        for t0 in range(0, seq_len, tc):
            kaug_ref[pi, t0:t0 + tc, pw:2 * pw] = bias_rows[:, t0:t0 + tc].T.astype(BF16)

    def load_queries(i, buf):
        q0 = pl.multiple_of(i * bq, bq)
        for pi in range(hp):
            qt = q_ref[0, pi * pw:(pi + 1) * pw, pl.ds(q0, bq)]
            zero = jnp.zeros_like(qt)
            qs_ref[buf, 2 * pi] = jnp.concatenate([jnp.where(first, qt, zero), sel[0]], axis=0)
            qs_ref[buf, 2 * pi + 1] = jnp.concatenate([jnp.where(first, zero, qt), sel[1]], axis=0)

    def scores(j, slot, buf):
        ks = pl.multiple_of(j * bk, bk)
        for h in range(2 * hp):
            s_ref[slot, h] = _dot(kaug_ref[h // 2, pl.ds(ks, bk), :], qs_ref[buf, h])

    def consume(j, slot, mask=None):
        ks = pl.multiple_of(j * bk, bk)
        for h in range(2 * hp):
            s = s_ref[slot, h]
            if mask is not None:
                s = jnp.where(mask, s, -jnp.inf)
            m = m_ref[h]
            mn = jnp.maximum(m, jnp.max(s, axis=0, keepdims=True))
            alpha = jnp.exp2(m - mn)
            pr = jnp.exp2(s - mn)
            m_ref[h] = mn
            l_ref[h] = l_ref[h] * alpha + jnp.sum(pr, axis=0, keepdims=True)
            vb = v_ref[0, h * FOX_HD:(h + 1) * FOX_HD, pl.ds(ks, bk)]
            acc_ref[h] = acc_ref[h] * alpha + _dot(vb, pr.astype(BF16))

    def q_block(i, buf):
        m_ref[...] = jnp.full_like(m_ref, -jnp.inf)
        l_ref[...] = jnp.zeros_like(l_ref)
        acc_ref[...] = jnp.zeros_like(acc_ref)

        def kv_pair(jj, _):
            j = 2 * jj
            scores(j + 1, 1, buf)
            consume(j, 0)
            scores(j + 2, 0, buf)
            consume(j + 1, 1)
            return 0

        lax.fori_loop(0, i, kv_pair, 0)
        scores(2 * i + 1, 1, buf)
        consume(2 * i, 0, diag_masks[0])
        load_queries(jnp.minimum(i + 1, nq - 1), 1 - buf)
        scores(0, 0, 1 - buf)
        consume(2 * i + 1, 1, diag_masks[1])
        q0 = pl.multiple_of(i * bq, bq)
        for pi in range(hp):
            pair_t = jnp.concatenate([acc_ref[2 * pi] / l_ref[2 * pi],
                                      acc_ref[2 * pi + 1] / l_ref[2 * pi + 1]], axis=0)
            o_ref[0, pl.ds(q0, bq), pi * pw:(pi + 1) * pw] = pair_t.T.astype(BF16)

    def q_pair(u, _):
        q_block(2 * u, 0)
        q_block(2 * u + 1, 1)
        return 0

    load_queries(0, 0)
    scores(0, 0, 0)
    lax.fori_loop(0, nq // 2, q_pair, 0)


def _fox_prompt(qft, kf, vft, ct, *, bq, hp):
    b, _, t = qft.shape
    pairs = FOX_H // 2
    assert t % (2 * bq) == 0 and pairs % hp == 0
    pw = 2 * FOX_HD
    tok = pl.BlockSpec((1, t, hp * pw), lambda i, p: (i, 0, p))
    feat = pl.BlockSpec((1, hp * pw, t), lambda i, p: (i, p, 0))
    return pl.pallas_call(
        functools.partial(_fox_prompt_kernel, seq_len=t, bq=bq, hp=hp),
        grid=(b, pairs // hp),
        in_specs=[feat, tok, feat, pl.BlockSpec((1, hp, 2, t), lambda i, p: (i, p, 0, 0))],
        out_specs=tok,
        out_shape=jax.ShapeDtypeStruct((b, t, D_FOX), BF16),
        scratch_shapes=[pltpu.VMEM((hp, t, 2 * pw), BF16),
                        pltpu.VMEM((2, 2 * hp, 2 * pw, bq), BF16),
                        pltpu.VMEM((2, 2 * hp, bq // 2, bq), F32),
                        pltpu.VMEM((2 * hp, 1, bq), F32),
                        pltpu.VMEM((2 * hp, 1, bq), F32),
                        pltpu.VMEM((2 * hp, FOX_HD, bq), F32)],
        compiler_params=_params(("arbitrary", "arbitrary")),
        name="fox_prompt",
    )(qft, kf, vft, ct)


def _lf_prep_kernel(lf_ref, m_ref, o_ref):
    hi, mid, lo = _split3(lf_ref[...])
    o_ref[...] = _dot(hi, m_ref[...]) + _dot(mid, m_ref[...]) + _dot(lo, m_ref[...])


def _lf_prep(lf_rows, *, tp):
    n, page = lf_rows.shape
    src = lax.broadcasted_iota(jnp.int32, (page, 2 * page), 0)
    dst = lax.broadcasted_iota(jnp.int32, (page, 2 * page), 1)
    mask = jnp.where(jnp.logical_or(dst >= page, src > dst), 1.0, 0.0).astype(BF16)
    assert n % tp == 0
    return pl.pallas_call(
        _lf_prep_kernel,
        grid=(n // tp,),
        in_specs=[pl.BlockSpec((tp, page), lambda i: (i, 0)), pl.BlockSpec(mask.shape, lambda i: (0, 0))],
        out_specs=pl.BlockSpec((tp, 2 * page), lambda i: (i, 0)),
        out_shape=jax.ShapeDtypeStruct((n, 2 * page), F32),
        compiler_params=_params(("arbitrary",)),
        name="lf_prep",
    )(lf_rows, mask)


def _fox_sample_kernel(pt_ref, q_ref, kn_ref, vn_ref, cc_ref, *rest, pp, seq, page):
    k_refs, v_refs, e_refs = rest[:pp], rest[pp:2 * pp], rest[2 * pp:3 * pp]
    o_ref, qbd_ref, m_ref, l_ref, acc_ref, car_ref = rest[3 * pp:]
    del pt_ref
    c = pl.program_id(1)
    rows = seq * FOX_H
    sub = lax.broadcasted_iota(jnp.int32, (FOX_H, D_FOX), 0)
    lane = lax.broadcasted_iota(jnp.int32, (FOX_H, D_FOX), 1)
    own = lax.shift_right_logical(lane, _log2(FOX_HD)) == sub

    @pl.when(c == 0)
    def _():
        q = q_ref[0]
        qbd_ref[...] = jnp.concatenate([jnp.where(own, q[i:i + 1, :], 0.0) for i in range(seq)], axis=0)
        m_ref[...] = jnp.full_like(m_ref, -jnp.inf)
        l_ref[...] = jnp.zeros_like(l_ref)
        acc_ref[...] = jnp.zeros_like(acc_ref)
        car_ref[...] = jnp.zeros_like(car_ref)

    qbd = qbd_ref[...]
    carry = car_ref[...]
    bias = [None] * pp
    for i in reversed(range(pp)):
        e = e_refs[i][0]
        bias[i] = jnp.concatenate([e[:, 0:page] + carry] * seq, axis=0)
        carry = carry + e[:, page:2 * page]
    car_ref[...] = carry
    kcat = jnp.concatenate([k_refs[i][0] for i in range(pp)], axis=1)
    s = _dot(qbd, kcat) + jnp.concatenate(bias, axis=1)
    m_old = m_ref[...]
    mn = jnp.maximum(m_old, jnp.max(s, axis=-1, keepdims=True))
    alpha = jnp.exp(m_old - mn)
    pr = jnp.exp(s - mn)
    l_ref[...] = l_ref[...] * alpha + jnp.sum(pr, axis=-1, keepdims=True)
    vcat = jnp.concatenate([v_refs[i][0] for i in range(pp)], axis=1)
    acc_ref[...] = acc_ref[...] * alpha + _dot_nt(pr, vcat)
    m_ref[...] = mn

    @pl.when(c == pl.num_programs(1) - 1)
    def _():
        kn = kn_ref[0]
        vn = vn_ref[0]
        cc = cc_ref[0]
        eye = (lax.broadcasted_iota(jnp.int32, (FOX_H, FOX_H), 0)
               == lax.broadcasted_iota(jnp.int32, (FOX_H, FOX_H), 1))
        q_of_row = lax.shift_right_logical(lax.broadcasted_iota(jnp.int32, (rows, 1), 0), _log2(FOX_H))
        s_new = []
        for j in range(seq):
            cj = jnp.sum(jnp.where(eye, cc[j:j + 1, :], 0.0), axis=-1, keepdims=True)
            sj = jnp.sum(qbd * kn[j:j + 1, :], axis=-1, keepdims=True) - jnp.concatenate([cj] * seq, axis=0)
            s_new.append(jnp.where(q_of_row >= j, sj, -jnp.inf))
        m_prev = m_ref[...]
        m_fin = m_prev
        for sj in s_new:
            m_fin = jnp.maximum(m_fin, sj)
        a_fin = jnp.exp(m_prev - m_fin)
        l_fin = l_ref[...] * a_fin
        acc = acc_ref[...] * a_fin
        for j, sj in enumerate(s_new):
            pj = jnp.exp(sj - m_fin)
            l_fin = l_fin + pj
            acc = acc + pj * vn[j:j + 1, :]
        out = acc / l_fin
        o_ref[0] = jnp.concatenate(
            [jnp.sum(jnp.where(own, out[i * FOX_H:(i + 1) * FOX_H, :], 0.0), axis=0, keepdims=True)
             for i in range(seq)], axis=0)


def _fox_sample(page_table, qf, kn, vn, cc, cache_kt, cache_vt, e_pages, *, pp):
    b, seq, _ = qf.shape
    n_pages = page_table.shape[1]
    assert n_pages % pp == 0
    page = cache_kt.shape[2]
    n_chunks = n_pages // pp
    tok = lambda w: pl.BlockSpec((1, seq, w), lambda i, c, pt: (i, 0, 0))

    def paged(shape, k):
        return pl.BlockSpec((1,) + shape, lambda i, c, pt: (pt[i, (n_chunks - 1 - c) * pp + k], 0, 0))

    rows = seq * FOX_H
    grid_spec = pltpu.PrefetchScalarGridSpec(
        num_scalar_prefetch=1,
        grid=(b, n_chunks),
        in_specs=[tok(D_FOX), tok(D_FOX), tok(D_FOX), tok(FOX_H)]
        + [paged((D_FOX, page), k) for k in range(pp)]
        + [paged((D_FOX, page), k) for k in range(pp)]
        + [paged((FOX_H, 2 * page), k) for k in range(pp)],
        out_specs=tok(D_FOX),
        scratch_shapes=[pltpu.VMEM((rows, D_FOX), F32), pltpu.VMEM((rows, 1), F32), pltpu.VMEM((rows, 1), F32),
                        pltpu.VMEM((rows, D_FOX), F32), pltpu.VMEM((FOX_H, page), F32)],
    )
    return pl.pallas_call(
        functools.partial(_fox_sample_kernel, pp=pp, seq=seq, page=page),
        grid_spec=grid_spec,
        out_shape=jax.ShapeDtypeStruct((b, seq, D_FOX), F32),
        compiler_params=_params(("arbitrary", "arbitrary")),
        name="fox_sample",
    )(page_table, qf, kn, vn, cc, *([cache_kt] * pp), *([cache_vt] * pp), *([e_pages] * pp))


def _out_kernel(x_ref, yg_ref, of_ref, zf_ref, w_ref, gf_ref, y_ref):
    y_ref[...] = _mix_out(x_ref[...], yg_ref[...], of_ref[...], zf_ref[...], w_ref, gf_ref)


def _out(x2d, yg, of, zf, w_out, g_final, *, tm):
    n, d = x2d.shape
    rows = lambda w: pl.BlockSpec((tm, w), lambda i: (i, 0))
    full = lambda a: pl.BlockSpec(a.shape, lambda i: (0, 0))
    return pl.pallas_call(
        _out_kernel,
        grid=(n // tm,),
        in_specs=[rows(d), rows(D_GLA), rows(D_FOX), rows(D_FOX), full(w_out), full(g_final)],
        out_specs=rows(d),
        out_shape=jax.ShapeDtypeStruct((n, d), F32),
        compiler_params=_params(("arbitrary",)),
        name="out",
    )(x2d, yg, of, zf, w_out, g_final)


def _regroup_weights(w_in, w_a2, b_a, b_f):
    o_alr = 2 * D_QK + 2 * D_GLA
    o_qf = o_alr + GLA_RANK
    o_fl = o_qf + 4 * D_FOX
    d = w_in.shape[0]
    w_main = jnp.concatenate([w_in[:, :o_alr], w_in[:, o_qf:o_fl]], axis=1).astype(BF16)
    w_kvt = w_in[:, o_qf:o_qf + 3 * D_FOX].T.astype(BF16)
    pad = LANE - FOX_H - GLA_RANK
    w_small = jnp.concatenate([w_in[:, o_fl:o_fl + FOX_H], w_in[:, o_alr:o_qf], jnp.zeros((d, pad), F32)],
                              axis=1).astype(BF16)
    w_a2p = jnp.concatenate([jnp.zeros((FOX_H, D_QK), F32), w_a2, jnp.zeros((pad, D_QK), F32)],
                            axis=0).astype(BF16)
    b_fp = jnp.concatenate([b_f, jnp.zeros((LANE - FOX_H,), F32)])[None, :]
    return w_main, w_kvt, w_small, w_a2p, b_a[None, :], b_fp


def kernel(x_prompt, x_sample, cache_k, cache_v, cache_logf, state_gla, page_table,
           g_norm, w_in, w_a2, b_a, b_f, gla_norm, w_out, g_final):
    depth = w_in.shape[0]
    assert depth == 1, "single-layer step"
    bp, tp, d = x_prompt.shape
    bs, ts, _ = x_sample.shape
    n_pool, page = cache_k.shape[1], cache_k.shape[2]
    l = 0
    weights = _regroup_weights(w_in[l], w_a2[l], b_a[l], b_f[l])
    g = g_norm[l][None, :]
    gn = gla_norm[l][None, :]
    w_o = w_out[l].astype(BF16)
    gf = g_final[None, :]

    xp = x_prompt.reshape(bp * tp, d)
    pr = _proj(xp, g, *weights, seq_len=tp, tm=min(512, tp))
    r3 = lambda a: a.reshape(bp, tp, a.shape[-1])
    of = _fox_prompt(pr["qft"], r3(pr["kfb"]), pr["vfbt"], pr["ct"].reshape(bp, FOX_H // 2, 2, tp),
                     bq=min(1024, tp // 2), hp=2)
    y_prompt, s_prompt = _gla_prompt(r3(pr["qg"]), r3(pr["kg"]), r3(pr["vg"]), r3(pr["zg"]), r3(pr["la"]), gn,
                                     x_prompt, of, r3(pr["zf"]), w_o, gf, tb=min(256, tp))
    heads_last = lambda a: a.reshape(bp, FOX_H, FOX_HD, tp).transpose(0, 3, 1, 2)[None]
    k_prompt, v_prompt = heads_last(pr["kft"]), heads_last(pr["vft"])
    lf_prompt = pr["lft"].transpose(0, 2, 1)[None]

    xs = x_sample.reshape(bs * ts, d)
    sm = _proj(xs, g, *weights, seq_len=ts, tm=min(512, bs * ts))
    yg_s, s_sample = _gla_sample(sm["qg"], sm["kg"], sm["vg"], sm["zg"], sm["la"], gn, state_gla[l], seq=ts, nb=8)
    ckt = cache_k[l].transpose(0, 2, 3, 1).reshape(n_pool, D_FOX, page)
    cvt = cache_v[l].transpose(0, 2, 3, 1).reshape(n_pool, D_FOX, page)
    clf = cache_logf[l].transpose(0, 2, 1).reshape(n_pool * FOX_H, page)
    e_pages = _lf_prep(clf, tp=math.gcd(n_pool * FOX_H, 2048)).reshape(n_pool, FOX_H, 2 * page)
    s3 = lambda a: a.reshape(bs, ts, a.shape[-1])
    of_s = _fox_sample(page_table, s3(sm["qf"].astype(F32)), s3(sm["kf"]), s3(sm["vf"]), s3(sm["cc"]),
                       ckt, cvt, e_pages, pp=math.gcd(page_table.shape[1], 32))
    y_sample = _out(xs, yg_s, of_s.reshape(bs * ts, D_FOX), sm["zf"], w_o, gf, tm=min(512, bs * ts))

    return (y_prompt, y_sample.reshape(bs, ts, d),
            k_prompt, v_prompt, lf_prompt, s_prompt[None],
            sm["kf"].reshape(1, bs, ts, FOX_H, FOX_HD), sm["vf"].reshape(1, bs, ts, FOX_H, FOX_HD),
            sm["lf"].reshape(1, bs, ts, FOX_H), s_sample[None])
```

```python
import functools
import math

import jax
import jax.numpy as jnp
from jax import lax
from jax.experimental import pallas as pl
from jax.experimental.pallas import tpu as pltpu

F32 = jnp.float32
BF16 = jnp.bfloat16

GLA_H = 4
GLA_DK = 64
GLA_DV = 128
GLA_RANK = 16
GLA_GATE_NORM = 16.0
GLA_CHUNK = 32
FOX_H = 8
FOX_HD = 64
RMS_EPS = 1e-6
_LOG2E = 1.4426950408889634

D_QK = GLA_H * GLA_DK
D_GLA = GLA_H * GLA_DV
D_FOX = FOX_H * FOX_HD
LANE = 128
SUBLANE = 8
VMEM_LIMIT = 56 * 1024 * 1024

_OFF_QG, _OFF_KG, _OFF_VG, _OFF_ZG = 0, 256, 512, 1024
_OFF_QF, _OFF_KF, _OFF_VF, _OFF_ZF, _OFF_END = 1536, 2048, 2560, 3072, 3584


def _dot(a, b):
    return jnp.dot(a, b, preferred_element_type=F32)


def _dot_nt(a, b):
    return lax.dot_general(a, b, (((1,), (1,)), ((), ())), preferred_element_type=F32)


def _dot_tn(a, b):
    return lax.dot_general(a, b, (((0,), (0,)), ((), ())), preferred_element_type=F32)


def _split3(x):
    hi = x.astype(BF16)
    r = x - hi.astype(F32)
    mid = r.astype(BF16)
    lo = (r - mid.astype(F32)).astype(BF16)
    return hi, mid, lo


def _dot_exact_mask(mask01, x):
    hi, mid, lo = _split3(x)
    return _dot(mask01, hi) + _dot(mask01, mid) + _dot(mask01, lo)


def _log_sigmoid(x):
    return jnp.minimum(x, 0.0) - jnp.log1p(jnp.exp(-jnp.abs(x)))


def _silu(x):
    return x / (1.0 + jnp.exp(-x))


def _log2(n):
    k = n.bit_length() - 1
    assert (1 << k) == n, f"{n} must be a power of two"
    return k


def _params(sem):
    return pltpu.CompilerParams(dimension_semantics=sem, vmem_limit_bytes=VMEM_LIMIT)


def _proj_kernel(x_ref, g_ref, wm_ref, wkvt_ref, ws_ref, wa_ref, ba_ref, bf_ref, *refs, tm, seq_len, names):
    out = dict(zip(names, refs))
    carry_ref = refs[len(names)]
    x = x_ref[...]
    ms = jnp.mean(x * x, axis=-1, keepdims=True)
    h = (x * lax.rsqrt(ms + RMS_EPS) * g_ref[...]).astype(BF16)

    def grp(a, b):
        return _dot(h, wm_ref[:, a:b])

    out["qg"][...] = (grp(_OFF_QG, _OFF_KG) * (GLA_DK ** -0.5)).astype(BF16)
    out["kg"][...] = grp(_OFF_KG, _OFF_VG).astype(BF16)
    out["vg"][...] = grp(_OFF_VG, _OFF_ZG).astype(BF16)
    out["zg"][...] = grp(_OFF_ZG, _OFF_QF).astype(BF16)
    out["zf"][...] = grp(_OFF_ZF, _OFF_END).astype(BF16)

    small = _dot(h, ws_ref[...])
    la = _log_sigmoid(_dot(small.astype(BF16), wa_ref[...]) + ba_ref[...])
    out["la"][...] = la * (1.0 / GLA_GATE_NORM)
    lf = _log_sigmoid(small + bf_ref[...])

    sub = min(tm, max(LANE, seq_len if seq_len < tm else LANE))
    row = lax.broadcasted_iota(jnp.int32, (sub, sub), 0)
    col = lax.broadcasted_iota(jnp.int32, (sub, sub), 1)
    keep = col <= row
    if seq_len < sub:
        sh = _log2(seq_len)
        keep = jnp.logical_and(keep, lax.shift_right_logical(row, sh) == lax.shift_right_logical(col, sh))
    tri = jnp.where(keep, 1.0, 0.0).astype(BF16)
    parts, run = [], None
    for r in range(tm // sub):
        part = _dot_exact_mask(tri, lf[r * sub:(r + 1) * sub])
        if seq_len > sub and run is not None:
            part = part + run
        run = part[sub - 1:sub, :]
        parts.append(part)
    c = jnp.concatenate(parts, axis=0)

    if seq_len >= tm:
        steps = seq_len // tm
        if steps > 1:
            @pl.when(pl.program_id(0) % steps == 0)
            def _():
                carry_ref[...] = jnp.zeros_like(carry_ref)

            c = c + carry_ref[0:1, :]
            carry_ref[0:1, :] = c[tm - 1:tm, :]
        qkvt = _dot_nt(wkvt_ref[...], h)
        out["qft"][0] = (qkvt[0:D_FOX] * (FOX_HD ** -0.5 * _LOG2E)).astype(BF16)
        out["kft"][0] = qkvt[D_FOX:2 * D_FOX]
        out["vft"][0] = qkvt[2 * D_FOX:3 * D_FOX]
        out["vfbt"][0] = qkvt[2 * D_FOX:3 * D_FOX].astype(BF16)
        out["kfb"][...] = grp(_OFF_KF, _OFF_VF).astype(BF16)
        out["lft"][0] = lf.T[0:FOX_H, :]
        out["ct"][0] = c.T[0:FOX_H, :]
    else:
        out["qf"][...] = (grp(_OFF_QF, _OFF_KF) * (FOX_HD ** -0.5)).astype(BF16)
        out["kf"][...] = grp(_OFF_KF, _OFF_VF)
        out["vf"][...] = grp(_OFF_VF, _OFF_ZF)
        out["lf"][...] = lf[:, 0:FOX_H]
        out["cc"][...] = c[:, 0:FOX_H]


def _proj(x2d, g, w_main, w_kvt, w_small, w_a2p, b_a, b_fp, *, seq_len, tm):
    n, d = x2d.shape
    assert n % tm == 0 and (seq_len % tm == 0 or tm % seq_len == 0)
    rows = lambda w: pl.BlockSpec((tm, w), lambda i: (i, 0))
    full = lambda a: pl.BlockSpec(a.shape, lambda i: (0, 0))
    tok = lambda w, dt: (jax.ShapeDtypeStruct((n, w), dt), rows(w))
    outs = {"qg": tok(D_QK, BF16), "kg": tok(D_QK, BF16), "vg": tok(D_GLA, BF16), "zg": tok(D_GLA, BF16),
            "la": tok(D_QK, F32), "zf": tok(D_FOX, BF16)}
    if seq_len >= tm:
        steps = seq_len // tm
        nseq = n // seq_len
        feat = lambda w, dt: (jax.ShapeDtypeStruct((nseq, w, seq_len), dt),
                              pl.BlockSpec((1, w, tm), lambda i: (i // steps, 0, i % steps)))
        outs.update({"qft": feat(D_FOX, BF16), "kfb": tok(D_FOX, BF16), "kft": feat(D_FOX, F32),
                     "vft": feat(D_FOX, F32), "vfbt": feat(D_FOX, BF16),
                     "lft": feat(FOX_H, F32), "ct": feat(FOX_H, F32)})
    else:
        outs.update({"qf": tok(D_FOX, BF16), "kf": tok(D_FOX, F32), "vf": tok(D_FOX, F32),
                     "lf": tok(FOX_H, F32), "cc": tok(FOX_H, F32)})
    names = tuple(outs)
    res = pl.pallas_call(
        functools.partial(_proj_kernel, tm=tm, seq_len=seq_len, names=names),
        grid=(n // tm,),
        in_specs=[rows(d), full(g), full(w_main), full(w_kvt), full(w_small), full(w_a2p), full(b_a), full(b_fp)],
        out_specs=[outs[k][1] for k in names],
        out_shape=[outs[k][0] for k in names],
        scratch_shapes=[pltpu.VMEM((SUBLANE, LANE), F32)],
        compiler_params=_params(("arbitrary",)),
        name="proj",
    )(x2d, g, w_main, w_kvt, w_small, w_a2p, b_a, b_fp)
    return dict(zip(names, res))


def _gla_prepare(q_ref, k_ref, la_ref, rows, chunk):
    la = la_ref[...].reshape(rows, D_QK)
    sh = _log2(chunk)
    row = lax.broadcasted_iota(jnp.int32, (rows, rows), 0)
    col = lax.broadcasted_iota(jnp.int32, (rows, rows), 1)
    same = lax.shift_right_logical(row, sh) == lax.shift_right_logical(col, sh)
    lower = jnp.logical_and(same, col <= row)
    upper = jnp.logical_and(same, col > row)
    cum = _dot_exact_mask(jnp.where(lower, 1.0, 0.0).astype(BF16), la)
    suf = _dot_exact_mask(jnp.where(upper, 1.0, 0.0).astype(BF16), la)
    q = q_ref[...].reshape(rows, D_QK).astype(F32)
    k = k_ref[...].reshape(rows, D_QK).astype(F32)
    qt = (q * jnp.exp(cum)).astype(BF16)
    kt = (k * jnp.exp(-cum)).astype(BF16)
    kp = (k * jnp.exp(suf)).astype(BF16)
    dec = jnp.exp(cum + suf)
    return qt, kt, kp, dec, lower


def _gla_intra(qtp, ktp, vp, lower):
    lane = lax.broadcasted_iota(jnp.int32, (1, 2 * GLA_DK), 1)
    first = lane < GLA_DK
    zero = jnp.zeros_like(qtp)
    a0 = jnp.where(lower, _dot_nt(jnp.where(first, qtp, zero), ktp), 0.0).astype(BF16)
    a1 = jnp.where(lower, _dot_nt(jnp.where(first, zero, qtp), ktp), 0.0).astype(BF16)
    vlane = lax.broadcasted_iota(jnp.int32, (1, 2 * GLA_DV), 1)
    vzero = jnp.zeros_like(vp)
    v0 = jnp.where(vlane < GLA_DV, vp, vzero)
    v1 = jnp.where(vlane < GLA_DV, vzero, vp)
    return _dot(a0, v0) + _dot(a1, v1)


def _pair_blockdiag_mask():
    r = lax.broadcasted_iota(jnp.int32, (2 * GLA_DV, 2 * GLA_DK), 0)
    c = lax.broadcasted_iota(jnp.int32, (2 * GLA_DV, 2 * GLA_DK), 1)
    return lax.shift_right_logical(r, _log2(GLA_DV)) == lax.shift_right_logical(c, _log2(GLA_DK))


def _gla_finish(o, z_ref, gn_ref, rows):
    ys = []
    for h in range(GLA_H):
        oh = o[:, h * GLA_DV:(h + 1) * GLA_DV]
        ms = jnp.mean(oh * oh, axis=-1, keepdims=True)
        ys.append(oh * lax.rsqrt(ms + RMS_EPS) * gn_ref[...])
    y = jnp.concatenate(ys, axis=1) * _silu(z_ref[...].reshape(rows, D_GLA).astype(F32))
    return y.astype(BF16)


def _mix_out(x, yg, of, zf, w_ref, gf_ref):
    yf = (of.astype(F32) * _silu(zf.astype(F32))).astype(BF16)
    d = _dot(yg, w_ref[0:D_GLA, :]) + _dot(yf, w_ref[D_GLA:D_GLA + D_FOX, :])
    xo = x + d
    ms = jnp.mean(xo * xo, axis=-1, keepdims=True)
    return xo * lax.rsqrt(ms + RMS_EPS) * gf_ref[...]


def _state_to_pairs(s_ref, idx, p):
    sa = s_ref[idx + (2 * p,)]
    sb = s_ref[idx + (2 * p + 1,)]
    z = jnp.zeros_like(sa)
    sbd = jnp.concatenate([jnp.concatenate([sa, z], axis=1), jnp.concatenate([z, sb], axis=1)], axis=0)
    return sbd.T


def _pairs_to_state(st, s_ref, idx, p):
    sbd = st.T
    s_ref[idx + (2 * p,)] = sbd[0:GLA_DK, 0:GLA_DV]
    s_ref[idx + (2 * p + 1,)] = sbd[GLA_DK:2 * GLA_DK, GLA_DV:2 * GLA_DV]


def _gla_prompt_kernel(q_ref, k_ref, v_ref, z_ref, la_ref, gn_ref, x_ref, of_ref, zf_ref, w_ref, gf_ref,
                       y_ref, sout_ref, st_ref, *, tb, chunk):
    t = pl.program_id(1)

    @pl.when(t == 0)
    def _():
        st_ref[...] = jnp.zeros_like(st_ref)

    qt, kt, kp, dec, lower = _gla_prepare(q_ref, k_ref, la_ref, tb, chunk)
    v = v_ref[0]
    bd = _pair_blockdiag_mask()
    o_parts = []
    for p in range(GLA_H // 2):
        ql = slice(p * 2 * GLA_DK, (p + 1) * 2 * GLA_DK)
        qtp, ktp, kpp = qt[:, ql], kt[:, ql], kp[:, ql]
        vp = v[:, p * 2 * GLA_DV:(p + 1) * 2 * GLA_DV]
        o_intra = _gla_intra(qtp, ktp, vp, lower)
        s = st_ref[p]
        inter = []
        for c in range(tb // chunk):
            sl = slice(c * chunk, (c + 1) * chunk)
            inter.append(_dot_nt(qtp[sl], s.astype(BF16)))
            upd = _dot_tn(vp[sl], kpp[sl])
            s = s * dec[c * chunk:c * chunk + 1, ql] + jnp.where(bd, upd, 0.0)
        st_ref[p] = s
        o_parts.append(o_intra + jnp.concatenate(inter, axis=0))
    yg = _gla_finish(jnp.concatenate(o_parts, axis=1), z_ref, gn_ref, tb)
    y_ref[0] = _mix_out(x_ref[0], yg, of_ref[0], zf_ref[0], w_ref, gf_ref)

    @pl.when(t == pl.num_programs(1) - 1)
    def _():
        for p in range(GLA_H // 2):
            _pairs_to_state(st_ref[p], sout_ref, (0,), p)


def _gla_prompt(qg, kg, vg, zg, la, gn, x, of, zf, w_out, g_final, *, tb):
    b, t, _ = qg.shape
    d = x.shape[-1]
    assert t % tb == 0 and tb % GLA_CHUNK == 0
    tok = lambda w: pl.BlockSpec((1, tb, w), lambda i, j: (i, j, 0))
    full = lambda a: pl.BlockSpec(a.shape, lambda i, j: (0,) * a.ndim)
    return pl.pallas_call(
        functools.partial(_gla_prompt_kernel, tb=tb, chunk=GLA_CHUNK),
        grid=(b, t // tb),
        in_specs=[tok(D_QK), tok(D_QK), tok(D_GLA), tok(D_GLA), tok(D_QK), full(gn),
                  tok(d), tok(D_FOX), tok(D_FOX), full(w_out), full(g_final)],
        out_specs=[tok(d), pl.BlockSpec((1, GLA_H, GLA_DK, GLA_DV), lambda i, j: (i, 0, 0, 0))],
        out_shape=[jax.ShapeDtypeStruct((b, t, d), F32),
                   jax.ShapeDtypeStruct((b, GLA_H, GLA_DK, GLA_DV), F32)],
        scratch_shapes=[pltpu.VMEM((GLA_H // 2, 2 * GLA_DV, 2 * GLA_DK), F32)],
        compiler_params=_params(("arbitrary", "arbitrary")),
        name="gla_prompt",
    )(qg, kg, vg, zg, la, gn, x, of, zf, w_out, g_final)


def _gla_sample_kernel(q_ref, k_ref, v_ref, z_ref, la_ref, gn_ref, s0_ref, y_ref, sout_ref, *, nb, seq):
    rows = nb * seq
    qt, kt, kp, dec, lower = _gla_prepare(q_ref, k_ref, la_ref, rows, seq)
    v = v_ref[...]
    bd = _pair_blockdiag_mask()
    seq_of_row = lax.shift_right_logical(lax.broadcasted_iota(jnp.int32, (rows, 1), 0), _log2(seq))
    o_parts = []
    for p in range(GLA_H // 2):
        ql = slice(p * 2 * GLA_DK, (p + 1) * 2 * GLA_DK)
        qtp, ktp, kpp = qt[:, ql], kt[:, ql], kp[:, ql]
        vp = v[:, p * 2 * GLA_DV:(p + 1) * 2 * GLA_DV]
        o = _gla_intra(qtp, ktp, vp, lower)
        for i in range(nb):
            mine = seq_of_row == i
            st = _state_to_pairs(s0_ref, (i,), p)
            o = o + jnp.where(mine, _dot_nt(qtp, st.astype(BF16)), 0.0)
            upd = _dot_tn(jnp.where(mine, vp, jnp.zeros_like(vp)), kpp)
            st = st * dec[i * seq:i * seq + 1, ql] + jnp.where(bd, upd, 0.0)
            _pairs_to_state(st, sout_ref, (i,), p)
        o_parts.append(o)
    y_ref[...] = _gla_finish(jnp.concatenate(o_parts, axis=1), z_ref, gn_ref, rows)


def _gla_sample(qg, kg, vg, zg, la, gn, s0, *, seq, nb):
    n = qg.shape[0]
    b = n // seq
    assert b % nb == 0 and (nb * seq) % SUBLANE == 0
    rows = nb * seq
    tok = lambda w: pl.BlockSpec((rows, w), lambda i: (i, 0))
    st = pl.BlockSpec((nb, GLA_H, GLA_DK, GLA_DV), lambda i: (i, 0, 0, 0))
    return pl.pallas_call(
        functools.partial(_gla_sample_kernel, nb=nb, seq=seq),
        grid=(b // nb,),
        in_specs=[tok(D_QK), tok(D_QK), tok(D_GLA), tok(D_GLA), tok(D_QK),
                  pl.BlockSpec((1, GLA_DV), lambda i: (0, 0)), st],
        out_specs=[tok(D_GLA), st],
        out_shape=[jax.ShapeDtypeStruct((n, D_GLA), BF16),
                   jax.ShapeDtypeStruct((b, GLA_H, GLA_DK, GLA_DV), F32)],
        compiler_params=_params(("arbitrary",)),
        name="gla_sample",
    )(qg, kg, vg, zg, la, gn, s0)


def _fox_prompt_kernel(q_ref, k_ref, v_ref, ct_ref, o_ref, kaug_ref, qs_ref, s_ref, m_ref, l_ref, acc_ref,
                       *, seq_len, bq, hp):
    pw = 2 * FOX_HD
    n_split = 3
    tc = min(512, seq_len)
    sub = lax.broadcasted_iota(jnp.int32, (pw, 1), 0)
    first = sub < FOX_HD
    bk = bq // 2
    nq = seq_len // bq
    key = lax.broadcasted_iota(jnp.int32, (bk, bq), 0)
    qry = lax.broadcasted_iota(jnp.int32, (bk, bq), 1)
    diag_masks = (key <= qry, key + bk <= qry)
    sel = [jnp.broadcast_to(jnp.where(jnp.logical_and(sub >= n_split * hl, sub < n_split * (hl + 1)),
                                      1.0, 0.0).astype(BF16), (pw, bq)) for hl in range(2)]

    for pi in range(hp):
        kaug_ref[pi, :, 0:pw] = k_ref[0, :, pi * pw:(pi + 1) * pw]
        parts = []
        for hl in range(2):
            parts.extend(p.astype(F32) for p in _split3(ct_ref[0, pi, hl:hl + 1, :] * (-_LOG2E)))
        parts.append(jnp.zeros((pw - 2 * n_split, seq_len), F32))
        bias_rows = jnp.concatenate(parts, axis=0)
        for t0 in range(0, seq_len, tc):
            kaug_ref[pi, t0:t0 + tc, pw:2 * pw] = bias_rows[:, t0:t0 + tc].T.astype(BF16)

    def load_queries(i, buf):
        q0 = pl.multiple_of(i * bq, bq)
        for pi in range(hp):
            qt = q_ref[0, pi * pw:(pi + 1) * pw, pl.ds(q0, bq)]
            zero = jnp.zeros_like(qt)
            qs_ref[buf, 2 * pi] = jnp.concatenate([jnp.where(first, qt, zero), sel[0]], axis=0)
            qs_ref[buf, 2 * pi + 1] = jnp.concatenate([jnp.where(first, zero, qt), sel[1]], axis=0)

    def scores(j, slot, buf):
        ks = pl.multiple_of(j * bk, bk)
        for h in range(2 * hp):
            s_ref[slot, h] = _dot(kaug_ref[h // 2, pl.ds(ks, bk), :], qs_ref[buf, h])

    def consume(j, slot, mask=None):
        ks = pl.multiple_of(j * bk, bk)
        for h in range(2 * hp):
            s = s_ref[slot, h]
            if mask is not None:
                s = jnp.where(mask, s, -jnp.inf)
            m = m_ref[h]
            mn = jnp.maximum(m, jnp.max(s, axis=0, keepdims=True))
            alpha = jnp.exp2(m - mn)
            pr = jnp.exp2(s - mn)
            m_ref[h] = mn
            l_ref[h] = l_ref[h] * alpha + jnp.sum(pr, axis=0, keepdims=True)
            vb = v_ref[0, h * FOX_HD:(h + 1) * FOX_HD, pl.ds(ks, bk)]
            acc_ref[h] = acc_ref[h] * alpha + _dot(vb, pr.astype(BF16))

    def q_block(i, buf):
        m_ref[...] = jnp.full_like(m_ref, -jnp.inf)
        l_ref[...] = jnp.zeros_like(l_ref)
        acc_ref[...] = jnp.zeros_like(acc_ref)

        def kv_pair(jj, _):
            j = 2 * jj
            scores(j + 1, 1, buf)
            consume(j, 0)
            scores(j + 2, 0, buf)
            consume(j + 1, 1)
            return 0

        lax.fori_loop(0, i, kv_pair, 0)
        scores(2 * i + 1, 1, buf)
        consume(2 * i, 0, diag_masks[0])
        load_queries(jnp.minimum(i + 1, nq - 1), 1 - buf)
        scores(0, 0, 1 - buf)
        consume(2 * i + 1, 1, diag_masks[1])
        q0 = pl.multiple_of(i * bq, bq)
        for pi in range(hp):
            pair_t = jnp.concatenate([acc_ref[2 * pi] / l_ref[2 * pi],
                                      acc_ref[2 * pi + 1] / l_ref[2 * pi + 1]], axis=0)
            o_ref[0, pl.ds(q0, bq), pi * pw:(pi + 1) * pw] = pair_t.T.astype(BF16)

    def q_pair(u, _):
        q_block(2 * u, 0)
        q_block(2 * u + 1, 1)
        return 0

    load_queries(0, 0)
    scores(0, 0, 0)
    lax.fori_loop(0, nq // 2, q_pair, 0)


def _fox_prompt(qft, kf, vft, ct, *, bq, hp):
    b, _, t = qft.shape
    pairs = FOX_H // 2
    assert t % (2 * bq) == 0 and pairs % hp == 0
    pw = 2 * FOX_HD
    tok = pl.BlockSpec((1, t, hp * pw), lambda i, p: (i, 0, p))
    feat = pl.BlockSpec((1, hp * pw, t), lambda i, p: (i, p, 0))
    return pl.pallas_call(
        functools.partial(_fox_prompt_kernel, seq_len=t, bq=bq, hp=hp),
        grid=(b, pairs // hp),
        in_specs=[feat, tok, feat, pl.BlockSpec((1, hp, 2, t), lambda i, p: (i, p, 0, 0))],
        out_specs=tok,
        out_shape=jax.ShapeDtypeStruct((b, t, D_FOX), BF16),
        scratch_shapes=[pltpu.VMEM((hp, t, 2 * pw), BF16),
                        pltpu.VMEM((2, 2 * hp, 2 * pw, bq), BF16),
                        pltpu.VMEM((2, 2 * hp, bq // 2, bq), F32),
                        pltpu.VMEM((2 * hp, 1, bq), F32),
                        pltpu.VMEM((2 * hp, 1, bq), F32),
                        pltpu.VMEM((2 * hp, FOX_HD, bq), F32)],
        compiler_params=_params(("arbitrary", "arbitrary")),
        name="fox_prompt",
    )(qft, kf, vft, ct)


def _lf_prep_kernel(lf_ref, m_ref, o_ref):
    hi, mid, lo = _split3(lf_ref[...])
    o_ref[...] = _dot(hi, m_ref[...]) + _dot(mid, m_ref[...]) + _dot(lo, m_ref[...])


def _lf_prep(lf_rows, *, tp):
    n, page = lf_rows.shape
    src = lax.broadcasted_iota(jnp.int32, (page, 2 * page), 0)
    dst = lax.broadcasted_iota(jnp.int32, (page, 2 * page), 1)
    mask = jnp.where(jnp.logical_or(dst >= page, src > dst), 1.0, 0.0).astype(BF16)
    assert n % tp == 0
    return pl.pallas_call(
        _lf_prep_kernel,
        grid=(n // tp,),
        in_specs=[pl.BlockSpec((tp, page), lambda i: (i, 0)), pl.BlockSpec(mask.shape, lambda i: (0, 0))],
        out_specs=pl.BlockSpec((tp, 2 * page), lambda i: (i, 0)),
        out_shape=jax.ShapeDtypeStruct((n, 2 * page), F32),
        compiler_params=_params(("arbitrary",)),
        name="lf_prep",
    )(lf_rows, mask)


def _fox_sample_kernel(pt_ref, q_ref, kn_ref, vn_ref, cc_ref, *rest, pp, seq, page):
    k_refs, v_refs, e_refs = rest[:pp], rest[pp:2 * pp], rest[2 * pp:3 * pp]
    o_ref, qbd_ref, m_ref, l_ref, acc_ref, car_ref = rest[3 * pp:]
    del pt_ref
    c = pl.program_id(1)
    rows = seq * FOX_H
    sub = lax.broadcasted_iota(jnp.int32, (FOX_H, D_FOX), 0)
    lane = lax.broadcasted_iota(jnp.int32, (FOX_H, D_FOX), 1)
    own = lax.shift_right_logical(lane, _log2(FOX_HD)) == sub

    @pl.when(c == 0)
    def _():
        q = q_ref[0]
        qbd_ref[...] = jnp.concatenate([jnp.where(own, q[i:i + 1, :], 0.0) for i in range(seq)], axis=0)
        m_ref[...] = jnp.full_like(m_ref, -jnp.inf)
        l_ref[...] = jnp.zeros_like(l_ref)
        acc_ref[...] = jnp.zeros_like(acc_ref)
        car_ref[...] = jnp.zeros_like(car_ref)

    qbd = qbd_ref[...]
    carry = car_ref[...]
    bias = [None] * pp
    for i in reversed(range(pp)):
        e = e_refs[i][0]
        bias[i] = jnp.concatenate([e[:, 0:page] + carry] * seq, axis=0)
        carry = carry + e[:, page:2 * page]
    car_ref[...] = carry
    kcat = jnp.concatenate([k_refs[i][0] for i in range(pp)], axis=1)
    s = _dot(qbd, kcat) + jnp.concatenate(bias, axis=1)
    m_old = m_ref[...]
    mn = jnp.maximum(m_old, jnp.max(s, axis=-1, keepdims=True))
    alpha = jnp.exp(m_old - mn)
    pr = jnp.exp(s - mn)
    l_ref[...] = l_ref[...] * alpha + jnp.sum(pr, axis=-1, keepdims=True)
    vcat = jnp.concatenate([v_refs[i][0] for i in range(pp)], axis=1)
    acc_ref[...] = acc_ref[...] * alpha + _dot_nt(pr, vcat)
    m_ref[...] = mn

    @pl.when(c == pl.num_programs(1) - 1)
    def _():
        kn = kn_ref[0]
        vn = vn_ref[0]
        cc = cc_ref[0]
        eye = (lax.broadcasted_iota(jnp.int32, (FOX_H, FOX_H), 0)
               == lax.broadcasted_iota(jnp.int32, (FOX_H, FOX_H), 1))
        q_of_row = lax.shift_right_logical(lax.broadcasted_iota(jnp.int32, (rows, 1), 0), _log2(FOX_H))
        s_new = []
        for j in range(seq):
            cj = jnp.sum(jnp.where(eye, cc[j:j + 1, :], 0.0), axis=-1, keepdims=True)
            sj = jnp.sum(qbd * kn[j:j + 1, :], axis=-1, keepdims=True) - jnp.concatenate([cj] * seq, axis=0)
            s_new.append(jnp.where(q_of_row >= j, sj, -jnp.inf))
        m_prev = m_ref[...]
        m_fin = m_prev
        for sj in s_new:
            m_fin = jnp.maximum(m_fin, sj)
        a_fin = jnp.exp(m_prev - m_fin)
        l_fin = l_ref[...] * a_fin
        acc = acc_ref[...] * a_fin
        for j, sj in enumerate(s_new):
            pj = jnp.exp(sj - m_fin)
            l_fin = l_fin + pj
            acc = acc + pj * vn[j:j + 1, :]
        out = acc / l_fin
        o_ref[0] = jnp.concatenate(
            [jnp.sum(jnp.where(own, out[i * FOX_H:(i + 1) * FOX_H, :], 0.0), axis=0, keepdims=True)
             for i in range(seq)], axis=0)


def _fox_sample(page_table, qf, kn, vn, cc, cache_kt, cache_vt, e_pages, *, pp):
    b, seq, _ = qf.shape
    n_pages = page_table.shape[1]
    assert n_pages % pp == 0
    page = cache_kt.shape[2]
    n_chunks = n_pages // pp
    tok = lambda w: pl.BlockSpec((1, seq, w), lambda i, c, pt: (i, 0, 0))

    def paged(shape, k):
        return pl.BlockSpec((1,) + shape, lambda i, c, pt: (pt[i, (n_chunks - 1 - c) * pp + k], 0, 0))

    rows = seq * FOX_H
    grid_spec = pltpu.PrefetchScalarGridSpec(
        num_scalar_prefetch=1,
        grid=(b, n_chunks),
        in_specs=[tok(D_FOX), tok(D_FOX), tok(D_FOX), tok(FOX_H)]
        + [paged((D_FOX, page), k) for k in range(pp)]
        + [paged((D_FOX, page), k) for k in range(pp)]
        + [paged((FOX_H, 2 * page), k) for k in range(pp)],
        out_specs=tok(D_FOX),
        scratch_shapes=[pltpu.VMEM((rows, D_FOX), F32), pltpu.VMEM((rows, 1), F32), pltpu.VMEM((rows, 1), F32),
                        pltpu.VMEM((rows, D_FOX), F32), pltpu.VMEM((FOX_H, page), F32)],
    )
    return pl.pallas_call(
        functools.partial(_fox_sample_kernel, pp=pp, seq=seq, page=page),
        grid_spec=grid_spec,
        out_shape=jax.ShapeDtypeStruct((b, seq, D_FOX), F32),
        compiler_params=_params(("arbitrary", "arbitrary")),
        name="fox_sample",
    )(page_table, qf, kn, vn, cc, *([cache_kt] * pp), *([cache_vt] * pp), *([e_pages] * pp))


def _out_kernel(x_ref, yg_ref, of_ref, zf_ref, w_ref, gf_ref, y_ref):
    y_ref[...] = _mix_out(x_ref[...], yg_ref[...], of_ref[...], zf_ref[...], w_ref, gf_ref)


def _out(x2d, yg, of, zf, w_out, g_final, *, tm):
    n, d = x2d.shape
    rows = lambda w: pl.BlockSpec((tm, w), lambda i: (i, 0))
    full = lambda a: pl.BlockSpec(a.shape, lambda i: (0, 0))
    return pl.pallas_call(
        _out_kernel,
        grid=(n // tm,),
        in_specs=[rows(d), rows(D_GLA), rows(D_FOX), rows(D_FOX), full(w_out), full(g_final)],
        out_specs=rows(d),
        out_shape=jax.ShapeDtypeStruct((n, d), F32),
        compiler_params=_params(("arbitrary",)),
        name="out",
    )(x2d, yg, of, zf, w_out, g_final)


def _regroup_weights(w_in, w_a2, b_a, b_f):
    o_alr = 2 * D_QK + 2 * D_GLA
    o_qf = o_alr + GLA_RANK
    o_fl = o_qf + 4 * D_FOX
    d = w_in.shape[0]
    w_main = jnp.concatenate([w_in[:, :o_alr], w_in[:, o_qf:o_fl]], axis=1).astype(BF16)
    w_kvt = w_in[:, o_qf:o_qf + 3 * D_FOX].T.astype(BF16)
    pad = LANE - FOX_H - GLA_RANK
    w_small = jnp.concatenate([w_in[:, o_fl:o_fl + FOX_H], w_in[:, o_alr:o_qf], jnp.zeros((d, pad), F32)],
                              axis=1).astype(BF16)
    w_a2p = jnp.concatenate([jnp.zeros((FOX_H, D_QK), F32), w_a2, jnp.zeros((pad, D_QK), F32)],
                            axis=0).astype(BF16)
    b_fp = jnp.concatenate([b_f, jnp.zeros((LANE - FOX_H,), F32)])[None, :]
    return w_main, w_kvt, w_small, w_a2p, b_a[None, :], b_fp


def kernel(x_prompt, x_sample, cache_k, cache_v, cache_logf, state_gla, page_table,
           g_norm, w_in, w_a2, b_a, b_f, gla_norm, w_out, g_final):
    depth = w_in.shape[0]
    assert depth == 1, "single-layer step"
    bp, tp, d = x_prompt.shape
    bs, ts, _ = x_sample.shape
    n_pool, page = cache_k.shape[1], cache_k.shape[2]
    l = 0
    weights = _regroup_weights(w_in[l], w_a2[l], b_a[l], b_f[l])
    g = g_norm[l][None, :]
    gn = gla_norm[l][None, :]
    w_o = w_out[l].astype(BF16)
    gf = g_final[None, :]

    xp = x_prompt.reshape(bp * tp, d)
    pr = _proj(xp, g, *weights, seq_len=tp, tm=min(1024, tp))
    r3 = lambda a: a.reshape(bp, tp, a.shape[-1])
    of = _fox_prompt(pr["qft"], r3(pr["kfb"]), pr["vfbt"], pr["ct"].reshape(bp, FOX_H // 2, 2, tp),
                     bq=min(1024, tp // 2), hp=2)
    y_prompt, s_prompt = _gla_prompt(r3(pr["qg"]), r3(pr["kg"]), r3(pr["vg"]), r3(pr["zg"]), r3(pr["la"]), gn,
                                     x_prompt, of, r3(pr["zf"]), w_o, gf, tb=min(256, tp))
    heads_last = lambda a: a.reshape(bp, FOX_H, FOX_HD, tp).transpose(0, 3, 1, 2)[None]
    k_prompt, v_prompt = heads_last(pr["kft"]), heads_last(pr["vft"])
    lf_prompt = pr["lft"].transpose(0, 2, 1)[None]

    xs = x_sample.reshape(bs * ts, d)
    sm = _proj(xs, g, *weights, seq_len=ts, tm=min(512, bs * ts))
    yg_s, s_sample = _gla_sample(sm["qg"], sm["kg"], sm["vg"], sm["zg"], sm["la"], gn, state_gla[l], seq=ts, nb=8)
    ckt = cache_k[l].transpose(0, 2, 3, 1).reshape(n_pool, D_FOX, page)
    cvt = cache_v[l].transpose(0, 2, 3, 1).reshape(n_pool, D_FOX, page)
    clf = cache_logf[l].transpose(0, 2, 1).reshape(n_pool * FOX_H, page)
    e_pages = _lf_prep(clf, tp=math.gcd(n_pool * FOX_H, 2048)).reshape(n_pool, FOX_H, 2 * page)
    s3 = lambda a: a.reshape(bs, ts, a.shape[-1])
    of_s = _fox_sample(page_table, s3(sm["qf"].astype(F32)), s3(sm["kf"]), s3(sm["vf"]), s3(sm["cc"]),
                       ckt, cvt, e_pages, pp=math.gcd(page_table.shape[1], 32))
    y_sample = _out(xs, yg_s, of_s.reshape(bs * ts, D_FOX), sm["zf"], w_o, gf, tm=min(512, bs * ts))

    return (y_prompt, y_sample.reshape(bs, ts, d),
            k_prompt, v_prompt, lf_prompt, s_prompt[None],
            sm["kf"].reshape(1, bs, ts, FOX_H, FOX_HD), sm["vf"].reshape(1, bs, ts, FOX_H, FOX_HD),
            sm["lf"].reshape(1, bs, ts, FOX_H), s_sample[None])
```
